```python
import jax, jax.numpy as jnp
from jax import lax
import numpy as np

D_MODEL = 2048
BATCH = 4
SEQ = 4096
DEPTH = 4

GRID_W = 64
CTX_LEN = 256
N_MIXERS = 3
MIX_CHUNK_MLP = 0
MIX_CONV = 1
MIX_ATTN = 2
EXPAND = 2
BRANCH_W = EXPAND * D_MODEL
CHUNK = 128
SGU_GROUPS = 16
SGU_GROUP_W = BRANCH_W // SGU_GROUPS
CONV_W = 31
HEAD_DIM = 128
N_HEADS = D_MODEL // HEAD_DIM
N_KV_HEADS = 4
GQA_GROUP = N_HEADS // N_KV_HEADS
ATTN_W = N_HEADS * HEAD_DIM
KV_W = N_KV_HEADS * HEAD_DIM
Q_BLOCK = 128
ROPE_THETA = 10000.0
ROPE_AXIS_DIM = HEAD_DIM // 2
DEEPNORM_ALPHA = (2 * DEPTH) ** 0.25
DEEPNORM_BETA = (8 * DEPTH) ** -0.25
LN_EPS = 1e-6

kernel_name = "hybrid_dit_interleaved_chunkmlp_conformer_gqa"


def _layer_norm(x, g, b):
    xf = x.astype(jnp.float32)
    mu = jnp.mean(xf, axis=-1, keepdims=True)
    var = jnp.mean(jnp.square(xf - mu), axis=-1, keepdims=True)
    y = (xf - mu) * lax.rsqrt(var + LN_EPS) * g.astype(jnp.float32) + b.astype(jnp.float32)
    return y.astype(x.dtype)


def _rms_norm(x, g):
    xf = x.astype(jnp.float32)
    y = xf * lax.rsqrt(jnp.mean(jnp.square(xf), axis=-1, keepdims=True) + LN_EPS) * g.astype(jnp.float32)
    return y.astype(x.dtype)


def _modulation(w, b, cond):
    m = jax.nn.silu(cond) @ w + b
    return jnp.split(m, 3, axis=-1)


def _axial_rope_tables(n_tokens):
    rows = n_tokens // GRID_W
    row = jnp.repeat(jnp.arange(rows, dtype=jnp.float32), GRID_W)
    col = jnp.tile(jnp.arange(GRID_W, dtype=jnp.float32), rows)
    inv = 1.0 / (ROPE_THETA ** (jnp.arange(0, ROPE_AXIS_DIM, 2, dtype=jnp.float32) / ROPE_AXIS_DIM))
    ang = jnp.concatenate([row[:, None] * inv, col[:, None] * inv], axis=-1)
    return jnp.cos(ang), jnp.sin(ang)


def _apply_rope(t, cos, sin):
    tf = t.astype(jnp.float32).reshape(t.shape[:-1] + (HEAD_DIM // 2, 2))
    t0, t1 = tf[..., 0], tf[..., 1]
    cs = cos[None, :, None, :]
    sn = sin[None, :, None, :]
    out = jnp.stack([t0 * cs - t1 * sn, t0 * sn + t1 * cs], axis=-1).reshape(t.shape)
    return out.astype(t.dtype)


def _gqa_softmax(q, k, v):
    s = jnp.einsum('bqkgd,bskd->bkgqs', q, k, preferred_element_type=jnp.float32) * (HEAD_DIM ** -0.5)
    p = jax.nn.softmax(s, axis=-1).astype(v.dtype)
    return jnp.einsum('bkgqs,bskd->bqkgd', p, v)


def _chunk_mlp(h, w_in, ln_g, ln_b, w_s, b_s, w_out):
    bsz, n, _ = h.shape
    u, v, g = jnp.split(h @ w_in, 3, axis=-1)
    v = _layer_norm(v, ln_g, ln_b)
    vb = v.reshape(bsz, n // CHUNK, CHUNK, SGU_GROUPS, SGU_GROUP_W)
    s = jnp.einsum('hpq,bcqhe->bcphe', w_s, vb) + b_s.T[:, :, None]
    s = s.reshape(bsz, n, BRANCH_W)
    return (u * s * jax.nn.silu(g)) @ w_out


def _conv_module(h, w_in, conv_w, conv_b, ln_g, ln_b, w_out):
    a, b, g = jnp.split(h @ w_in, 3, axis=-1)
    y = a * jax.nn.sigmoid(b)
    y = lax.conv_general_dilated(
        y, conv_w[:, None, :].astype(y.dtype), window_strides=(1,),
        padding=[(CONV_W // 2, CONV_W // 2)], dimension_numbers=('NWC', 'WIO', 'NWC'),
        feature_group_count=BRANCH_W) + conv_b
    y = jax.nn.silu(_layer_norm(y, ln_g, ln_b))
    return (y * jax.nn.silu(g)) @ w_out


def _attn_qkvg(h, w_in, q_g, k_g):
    bsz, n, _ = h.shape
    q, k, v, g = jnp.split(h @ w_in, [ATTN_W, ATTN_W + KV_W, ATTN_W + 2 * KV_W], axis=-1)
    q = _rms_norm(q.reshape(bsz, n, N_HEADS, HEAD_DIM), q_g)
    k = _rms_norm(k.reshape(bsz, n, N_KV_HEADS, HEAD_DIM), k_g)
    v = v.reshape(bsz, n, N_KV_HEADS, HEAD_DIM)
    return q, k, v, g


def _ctx_kv(hc, w_in, k_g):
    bsz, n, _ = hc.shape
    k, v = jnp.split(hc @ w_in[:, ATTN_W:ATTN_W + 2 * KV_W], 2, axis=-1)
    k = _rms_norm(k.reshape(bsz, n, N_KV_HEADS, HEAD_DIM), k_g)
    return k, v.reshape(bsz, n, N_KV_HEADS, HEAD_DIM)


def _attention_layer(h, hc, w_in, q_g, k_g, w_out, cos, sin, need_ctx_out):
    bsz, n, _ = h.shape
    q, k, v, g = _attn_qkvg(h, w_in, q_g, k_g)
    q = _apply_rope(q, cos, sin)
    k = _apply_rope(k, cos, sin)
    if need_ctx_out:
        qc, kc, vc, gc = _attn_qkvg(hc, w_in, q_g, k_g)
    else:
        kc, vc = _ctx_kv(hc, w_in, k_g)
    k_all = jnp.concatenate([k, kc], axis=1)
    v_all = jnp.concatenate([v, vc], axis=1)
    n_blk = n // Q_BLOCK
    qb = q.reshape(bsz, n_blk, Q_BLOCK, N_KV_HEADS, GQA_GROUP, HEAD_DIM).transpose(1, 0, 2, 3, 4, 5)
    ob = lax.map(lambda qi: _gqa_softmax(qi, k_all, v_all), qb)
    o = ob.transpose(1, 0, 2, 3, 4, 5).reshape(bsz, n, ATTN_W)
    y = (o * jax.nn.silu(g)) @ w_out
    yc = None
    if need_ctx_out:
        lc = hc.shape[1]
        qc5 = qc.reshape(bsz, lc, N_KV_HEADS, GQA_GROUP, HEAD_DIM)
        oc = _gqa_softmax(qc5, kc, vc).reshape(bsz, lc, ATTN_W)
        yc = (oc * jax.nn.silu(gc)) @ w_out
    return y, yc


def setup_inputs(seed: int = 0) -> dict:
    key = jax.random.key(seed)
    ks = jax.random.split(key, 32)
    kinds = [i % N_MIXERS for i in range(DEPTH)]
    n_a = kinds.count(MIX_CHUNK_MLP)
    n_b = kinds.count(MIX_CONV)
    n_c = kinds.count(MIX_ATTN)
    D = D_MODEL
    E = BRANCH_W
    f32 = jnp.float32

    def nrm(k, shape, scale):
        return jax.random.normal(k, shape, f32) * scale

    c_w_in = jnp.concatenate([
        nrm(ks[20], (n_c, D, ATTN_W), D ** -0.5),
        nrm(ks[21], (n_c, D, KV_W), D ** -0.5),
        nrm(ks[22], (n_c, D, KV_W), D ** -0.5 * DEEPNORM_BETA),
        nrm(ks[23], (n_c, D, ATTN_W), D ** -0.5)], axis=-1)
    return {
        "x": nrm(ks[0], (BATCH, SEQ, D), 1.0),
        "c": nrm(ks[1], (BATCH, D), 1.0),
        "ctx": nrm(ks[2], (BATCH, CTX_LEN, D), 1.0),
        "c_ctx": nrm(ks[3], (D,), 1.0),
        "mod_w": nrm(ks[4], (DEPTH, D, 3 * D), 0.5 * D ** -0.5),
        "mod_b": nrm(ks[5], (DEPTH, 3 * D), 0.02),
        "post_g": 1.0 + nrm(ks[6], (DEPTH, D), 0.02),
        "post_b": nrm(ks[7], (DEPTH, D), 0.02),
        "a_w_in": nrm(ks[8], (n_a, D, 3 * E), D ** -0.5),
        "a_ln_g": 1.0 + nrm(ks[9], (n_a, E), 0.02),
        "a_ln_b": nrm(ks[10], (n_a, E), 0.02),
        "a_w_s": nrm(ks[11], (n_a, SGU_GROUPS, CHUNK, CHUNK), CHUNK ** -0.5),
        "a_b_s": 1.0 + nrm(ks[12], (n_a, SGU_GROUPS, CHUNK), 0.02),
        "a_w_out": nrm(ks[13], (n_a, E, D), E ** -0.5 * DEEPNORM_BETA),
        "b_w_in": nrm(ks[14], (n_b, D, 3 * E), D ** -0.5),
        "b_conv_w": nrm(ks[15], (n_b, CONV_W, E), CONV_W ** -0.5),
        "b_conv_b": nrm(ks[16], (n_b, E), 0.02),
        "b_ln_g": 1.0 + nrm(ks[17], (n_b, E), 0.02),
        "b_ln_b": nrm(ks[18], (n_b, E), 0.02),
        "b_w_out": nrm(ks[19], (n_b, E, D), E ** -0.5 * DEEPNORM_BETA),
        "c_w_in": c_w_in,
        "c_q_g": 1.0 + nrm(ks[24], (n_c, HEAD_DIM), 0.02),
        "c_k_g": 1.0 + nrm(ks[25], (n_c, HEAD_DIM), 0.02),
        "c_w_out": nrm(ks[26], (n_c, ATTN_W, D), ATTN_W ** -0.5 * DEEPNORM_BETA),
    }


def reference(x, c, ctx, c_ctx, mod_w, mod_b, post_g, post_b,
              a_w_in, a_ln_g, a_ln_b, a_w_s, a_b_s, a_w_out,
              b_w_in, b_conv_w, b_conv_b, b_ln_g, b_ln_b, b_w_out,
              c_w_in, c_q_g, c_k_g, c_w_out):
    n_tokens = x.shape[1]
    cos, sin = _axial_rope_tables(n_tokens)
    kinds = [i % N_MIXERS for i in range(DEPTH)]
    for i in range(DEPTH):
        kind = kinds[i]
        slot = kinds[:i].count(kind)
        ctx_read_later = MIX_ATTN in kinds[i + 1:]
        shift, scale, gate = _modulation(mod_w[i], mod_b[i], c)
        h = x * (1.0 + scale[:, None, :]) + shift[:, None, :]
        hc = None
        gate_c = None
        if ctx_read_later or kind == MIX_ATTN:
            shift_c, scale_c, gate_c = _modulation(mod_w[i], mod_b[i], c_ctx)
            hc = ctx * (1.0 + scale_c) + shift_c
        yc = None
        if kind == MIX_CHUNK_MLP:
            prm = (a_w_in[slot], a_ln_g[slot], a_ln_b[slot], a_w_s[slot], a_b_s[slot], a_w_out[slot])
            y = _chunk_mlp(h, *prm)
            if ctx_read_later:
                yc = _chunk_mlp(hc, *prm)
        elif kind == MIX_CONV:
            prm = (b_w_in[slot], b_conv_w[slot], b_conv_b[slot], b_ln_g[slot], b_ln_b[slot], b_w_out[slot])
            y = _conv_module(h, *prm)
            if ctx_read_later:
                yc = _conv_module(hc, *prm)
        else:
            y, yc = _attention_layer(h, hc, c_w_in[slot], c_q_g[slot], c_k_g[slot], c_w_out[slot],
                                     cos, sin, ctx_read_later)
        x = _layer_norm(DEEPNORM_ALPHA * x + gate[:, None, :] * y, post_g[i], post_b[i])
        if ctx_read_later:
            ctx = _layer_norm(DEEPNORM_ALPHA * ctx + gate_c * yc, post_g[i], post_b[i])
    return x
```

```python
import functools

import jax
import jax.numpy as jnp
from jax import lax
from jax.experimental import pallas as pl
from jax.experimental.pallas import tpu as pltpu

F32 = jnp.float32
BF16 = jnp.bfloat16

DEPTH = 4
N_MIXERS = 3
GRID_W = 64
CHUNK = 128
SGU_GROUPS = 16
CONV_W = 31
CONV_PAD = CONV_W // 2
HALO = 16
HEAD_DIM = 128
N_HEADS = 16
N_KV_HEADS = 4
GQA_GROUP = N_HEADS // N_KV_HEADS
ROPE_THETA = 10000.0
DEEPNORM_ALPHA = (2 * DEPTH) ** 0.25
LN_EPS = 1e-6
MOD_ROWS = 8

VMEM_LIMIT = 56 * 1024 * 1024


def _cparams(sem):
    return pltpu.CompilerParams(dimension_semantics=sem, vmem_limit_bytes=VMEM_LIMIT)


def _layer_norm(x, g, b):
    mu = jnp.mean(x, axis=-1, keepdims=True)
    xc = x - mu
    var = jnp.mean(xc * xc, axis=-1, keepdims=True)
    return xc * lax.rsqrt(var + LN_EPS) * g + b


def _silu(x):
    return x * jax.nn.sigmoid(x)


def _mod_kernel(c_ref, w_ref, b_ref, o_ref):
    s = _silu(c_ref[...]).astype(BF16)
    o_ref[0] = jnp.dot(s, w_ref[0].astype(BF16), preferred_element_type=F32) + b_ref[0]


def _modulation(cond, mod_w, mod_b):
    depth, d, n = mod_w.shape
    bn = 1024
    return pl.pallas_call(
        _mod_kernel,
        grid=(depth, n // bn),
        in_specs=[
            pl.BlockSpec((MOD_ROWS, d), lambda i, j: (0, 0)),
            pl.BlockSpec((1, d, bn), lambda i, j: (i, 0, j)),
            pl.BlockSpec((1, 1, bn), lambda i, j: (i, 0, j)),
        ],
        out_specs=pl.BlockSpec((1, MOD_ROWS, bn), lambda i, j: (i, 0, j)),
        out_shape=jax.ShapeDtypeStruct((depth, MOD_ROWS, n), F32),
        compiler_params=_cparams(("arbitrary", "arbitrary")),
        name="modulation",
    )(cond, mod_w, mod_b.reshape(depth, 1, n))


def _mm_kernel(x_ref, sc_ref, sh_ref, w_ref, o_ref, h_ref):
    @pl.when(pl.program_id(2) == 0)
    def _():
        h_ref[...] = (x_ref[0] * (1.0 + sc_ref[0]) + sh_ref[0]).astype(BF16)

    o_ref[0] = jnp.dot(h_ref[...], w_ref[...], preferred_element_type=F32).astype(o_ref.dtype)


def _mod_matmul(x, scale, shift, w, bm=1024, bn=1024):
    bsz, s, d = x.shape
    n = w.shape[1]
    bm = min(bm, s)
    bn = min(bn, n)
    return pl.pallas_call(
        _mm_kernel,
        grid=(bsz, s // bm, n // bn),
        in_specs=[
            pl.BlockSpec((1, bm, d), lambda b, i, j: (b, i, 0)),
            pl.BlockSpec((1, 1, d), lambda b, i, j: (b, 0, 0)),
            pl.BlockSpec((1, 1, d), lambda b, i, j: (b, 0, 0)),
            pl.BlockSpec((d, bn), lambda b, i, j: (0, j)),
        ],
        out_specs=pl.BlockSpec((1, bm, bn), lambda b, i, j: (b, i, j)),
        out_shape=jax.ShapeDtypeStruct((bsz, s, n), F32),
        scratch_shapes=[pltpu.VMEM((bm, d), BF16)],
        compiler_params=_cparams(("arbitrary", "arbitrary", "arbitrary")),
        name="mod_matmul",
    )(x, scale, shift, w)


def _out_kernel(z_ref, w_ref, x_ref, gate_ref, pg_ref, pb_ref, o_ref, acc_ref):
    k = pl.program_id(2)

    @pl.when(k == 0)
    def _():
        acc_ref[...] = jnp.zeros_like(acc_ref)

    acc_ref[...] += jnp.dot(z_ref[0], w_ref[...], preferred_element_type=F32)

    @pl.when(k == pl.num_programs(2) - 1)
    def _():
        r = DEEPNORM_ALPHA * x_ref[0] + gate_ref[0] * acc_ref[...]
        o_ref[0] = _layer_norm(r, pg_ref[...], pb_ref[...])


def _out_matmul_norm(z, w, x, gate, post_g, post_b, bm=512, bk=1024):
    bsz, s, kdim = z.shape
    d = w.shape[1]
    bm = min(bm, s)
    bk = min(bk, kdim)
    return pl.pallas_call(
        _out_kernel,
        grid=(bsz, s // bm, kdim // bk),
        in_specs=[
            pl.BlockSpec((1, bm, bk), lambda b, i, k: (b, i, k)),
            pl.BlockSpec((bk, d), lambda b, i, k: (k, 0)),
            pl.BlockSpec((1, bm, d), lambda b, i, k: (b, i, 0)),
            pl.BlockSpec((1, 1, d), lambda b, i, k: (b, 0, 0)),
            pl.BlockSpec((1, d), lambda b, i, k: (0, 0)),
            pl.BlockSpec((1, d), lambda b, i, k: (0, 0)),
        ],
        out_specs=pl.BlockSpec((1, bm, d), lambda b, i, k: (b, i, 0)),
        out_shape=jax.ShapeDtypeStruct((bsz, s, d), F32),
        scratch_shapes=[pltpu.VMEM((bm, d), F32)],
        compiler_params=_cparams(("arbitrary", "arbitrary", "arbitrary")),
        name="out_matmul_norm",
    )(z, w, x, gate, post_g.reshape(1, d), post_b.reshape(1, d))


def _sgu_kernel(u_ref, v_ref, g_ref, lg_ref, lb_ref, ws_ref, bs_ref, z_ref, vn_ref):
    bm = v_ref.shape[1]
    gw = v_ref.shape[2] // SGU_GROUPS
    for c in range(bm // CHUNK):
        rows = slice(c * CHUNK, (c + 1) * CHUNK)
        vn_ref[...] = _layer_norm(v_ref[0, rows, :], lg_ref[...], lb_ref[...]).astype(BF16)
        for h in range(SGU_GROUPS):
            cols = slice(h * gw, (h + 1) * gw)
            s = jnp.dot(ws_ref[h], vn_ref[:, cols], preferred_element_type=F32) + bs_ref[:, h:h + 1]
            z_ref[0, rows, cols] = (u_ref[0, rows, cols] * s * _silu(g_ref[0, rows, cols])).astype(BF16)


def _sgu_gate(uvg, ln_g, ln_b, w_s, b_s_t, bm=256):
    bsz, s, n3 = uvg.shape
    e = n3 // 3
    bm = min(bm, s)
    return pl.pallas_call(
        _sgu_kernel,
        grid=(bsz, s // bm),
        in_specs=[
            pl.BlockSpec((1, bm, e), lambda b, i: (b, i, 0)),
            pl.BlockSpec((1, bm, e), lambda b, i: (b, i, 1)),
            pl.BlockSpec((1, bm, e), lambda b, i: (b, i, 2)),
            pl.BlockSpec((1, e), lambda b, i: (0, 0)),
            pl.BlockSpec((1, e), lambda b, i: (0, 0)),
            pl.BlockSpec((SGU_GROUPS, CHUNK, CHUNK), lambda b, i: (0, 0, 0)),
            pl.BlockSpec((CHUNK, SGU_GROUPS), lambda b, i: (0, 0)),
        ],
        out_specs=pl.BlockSpec((1, bm, e), lambda b, i: (b, i, 0)),
        out_shape=jax.ShapeDtypeStruct((bsz, s, e), BF16),
        scratch_shapes=[pltpu.VMEM((CHUNK, e), BF16)],
        compiler_params=_cparams(("arbitrary", "arbitrary")),
        name="sgu_gate",
    )(uvg, uvg, uvg, ln_g.reshape(1, e), ln_b.reshape(1, e), w_s, b_s_t)


CONV_ROWS = 128
CONV_COLS = 256
NORM_ROWS = 32


def _conv_kernel(a_ref, b_ref, g_ref, ap_ref, bp_ref, an_ref, bn_ref,
                 cw_ref, cb_ref, lg_ref, lb_ref, z_ref, yext_ref, yc_ref):
    i = pl.program_id(1)
    bm = a_ref.shape[1]
    e = a_ref.shape[2]
    yext_ref[HALO:HALO + bm, :] = a_ref[0] * jax.nn.sigmoid(b_ref[0])
    yp = ap_ref[0] * jax.nn.sigmoid(bp_ref[0])
    yext_ref[0:HALO, :] = jnp.where(i > 0, yp, 0.0)
    yn = an_ref[0] * jax.nn.sigmoid(bn_ref[0])
    yext_ref[HALO + bm:HALO + bm + HALO, :] = jnp.where(i < pl.num_programs(1) - 1, yn, 0.0)

    rc = min(CONV_ROWS, bm)
    for r0 in range(0, bm, rc):
        for c0 in range(0, e, CONV_COLS):
            cols = slice(c0, c0 + CONV_COLS)
            acc = jnp.broadcast_to(cb_ref[:, cols], (rc, CONV_COLS))
            for k in range(CONV_W):
                lo = r0 + HALO - CONV_PAD + k
                acc = acc + cw_ref[k:k + 1, cols] * yext_ref[lo:lo + rc, cols]
            yc_ref[r0:r0 + rc, cols] = acc

    def norm_tile(t, carry):
        rows = pl.ds(pl.multiple_of(t * NORM_ROWS, NORM_ROWS), NORM_ROWS)
        y = _silu(_layer_norm(yc_ref[rows, :], lg_ref[...], lb_ref[...]))
        z_ref[0, rows, :] = (y * _silu(g_ref[0, rows, :])).astype(BF16)
        return carry

    lax.fori_loop(0, bm // NORM_ROWS, norm_tile, 0)


def _conv_gate(abg, conv_w, conv_b, ln_g, ln_b, bm=128):
    bsz, s, n3 = abg.shape
    e = n3 // 3
    bm = min(bm, s)
    hb = bm // HALO
    last = s // HALO - 1

    def prev_map(col):
        return lambda b, i: (b, jnp.maximum(i * hb - 1, 0), col)

    def next_map(col):
        return lambda b, i: (b, jnp.minimum((i + 1) * hb, last), col)

    return pl.pallas_call(
        _conv_kernel,
        grid=(bsz, s // bm),
        in_specs=[
            pl.BlockSpec((1, bm, e), lambda b, i: (b, i, 0)),
            pl.BlockSpec((1, bm, e), lambda b, i: (b, i, 1)),
            pl.BlockSpec((1, bm, e), lambda b, i: (b, i, 2)),
            pl.BlockSpec((1, HALO, e), prev_map(0)),
            pl.BlockSpec((1, HALO, e), prev_map(1)),
            pl.BlockSpec((1, HALO, e), next_map(0)),
            pl.BlockSpec((1, HALO, e), next_map(1)),
            pl.BlockSpec((CONV_W, e), lambda b, i: (0, 0)),
            pl.BlockSpec((1, e), lambda b, i: (0, 0)),
            pl.BlockSpec((1, e), lambda b, i: (0, 0)),
            pl.BlockSpec((1, e), lambda b, i: (0, 0)),
        ],
        out_specs=pl.BlockSpec((1, bm, e), lambda b, i: (b, i, 0)),
        out_shape=jax.ShapeDtypeStruct((bsz, s, e), BF16),
        scratch_shapes=[pltpu.VMEM((bm + 2 * HALO, e), F32), pltpu.VMEM((bm, e), F32)],
        compiler_params=_cparams(("arbitrary", "arbitrary")),
        name="conv_gate",
    )(abg, abg, abg, abg, abg, abg, abg, conv_w, conv_b.reshape(1, e),
      ln_g.reshape(1, e), ln_b.reshape(1, e))


def _norm_rope_head(t, gain, cosf, sinf, even):
    ms = jnp.mean(t * t, axis=-1, keepdims=True)
    tn = t * lax.rsqrt(ms + LN_EPS) * gain
    swapped = jnp.where(even, pltpu.roll(tn, HEAD_DIM - 1, 1), pltpu.roll(tn, 1, 1))
    return tn * cosf + swapped * sinf


def _qk_kernel(nq, *refs):
    if nq:
        q_ref, k_ref, v_ref, cos_ref, sin_ref, qg_ref, kg_ref, qo_ref, ko_ref, vo_ref = refs
    else:
        k_ref, v_ref, cos_ref, sin_ref, qg_ref, kg_ref, ko_ref, vo_ref = refs
    cosf = cos_ref[...]
    sinf = sin_ref[...]
    even = (lax.broadcasted_iota(jnp.int32, cosf.shape, 1) % 2) == 0
    for h in range(nq):
        cols = slice(h * HEAD_DIM, (h + 1) * HEAD_DIM)
        qh = _norm_rope_head(q_ref[0, :, cols], qg_ref[...], cosf, sinf, even)
        qo_ref[0, :, cols] = (qh * (HEAD_DIM ** -0.5)).astype(BF16)
    for h in range(N_KV_HEADS):
        cols = slice(h * HEAD_DIM, (h + 1) * HEAD_DIM)
        ko_ref[0, :, cols] = _norm_rope_head(k_ref[0, :, cols], kg_ref[...], cosf, sinf, even).astype(BF16)
    vo_ref[0] = v_ref[0].astype(BF16)


def _qk_norm_rope(proj, cosf, sinf, q_g, k_g, with_q, bm=256):
    bsz, s, _ = proj.shape
    bm = min(bm, s)
    aw = N_HEADS * HEAD_DIM
    kw = N_KV_HEADS * HEAD_DIM
    koff = (aw // kw) if with_q else 0
    in_specs = []
    out_specs = []
    out_shape = []
    args = []
    if with_q:
        in_specs.append(pl.BlockSpec((1, bm, aw), lambda b, i: (b, i, 0)))
        args.append(proj)
        out_specs.append(pl.BlockSpec((1, bm, aw), lambda b, i: (b, i, 0)))
        out_shape.append(jax.ShapeDtypeStruct((bsz, s, aw), BF16))
    in_specs += [
        pl.BlockSpec((1, bm, kw), lambda b, i: (b, i, koff)),
        pl.BlockSpec((1, bm, kw), lambda b, i: (b, i, koff + 1)),
        pl.BlockSpec((bm, HEAD_DIM), lambda b, i: (i, 0)),
        pl.BlockSpec((bm, HEAD_DIM), lambda b, i: (i, 0)),
        pl.BlockSpec((1, HEAD_DIM), lambda b, i: (0, 0)),
        pl.BlockSpec((1, HEAD_DIM), lambda b, i: (0, 0)),
    ]
    args += [proj, proj, cosf, sinf, q_g.reshape(1, HEAD_DIM), k_g.reshape(1, HEAD_DIM)]
    out_specs += [pl.BlockSpec((1, bm, kw), lambda b, i: (b, i, 0))] * 2
    out_shape += [jax.ShapeDtypeStruct((bsz, s, kw), BF16)] * 2
    return pl.pallas_call(
        functools.partial(_qk_kernel, N_HEADS if with_q else 0),
        grid=(bsz, s // bm),
        in_specs=in_specs,
        out_specs=out_specs,
        out_shape=out_shape,
        compiler_params=_cparams(("arbitrary", "arbitrary")),
        name="qk_norm_rope",
    )(*args)


def _attn_kernel(q_ref, k_ref, v_ref, g_ref, z_ref):
    k = k_ref[0]
    v = v_ref[0]
    for h in range(GQA_GROUP):
        cols = slice(h * HEAD_DIM, (h + 1) * HEAD_DIM)
        s = lax.dot_general(q_ref[0, :, cols], k, (((1,), (1,)), ((), ())), preferred_element_type=F32)
        m = jnp.max(s, axis=-1, keepdims=True)
        p = jnp.exp(s - m)
        l = jnp.sum(p, axis=-1, keepdims=True)
        o = jnp.dot(p.astype(BF16), v, preferred_element_type=F32) / l
        z_ref[0, :, cols] = (o * _silu(g_ref[0, :, cols])).astype(BF16)


def _attention(q, k_all, v_all, proj, bq=256):
    bsz, s, aw = q.shape
    lk = k_all.shape[1]
    gw = GQA_GROUP * HEAD_DIM
    goff = (proj.shape[2] - aw) // gw
    bq = min(bq, s)
    return pl.pallas_call(
        _attn_kernel,
        grid=(bsz, N_KV_HEADS, s // bq),
        in_specs=[
            pl.BlockSpec((1, bq, gw), lambda b, h, i: (b, i, h)),
            pl.BlockSpec((1, lk, HEAD_DIM), lambda b, h, i: (b, 0, h)),
            pl.BlockSpec((1, lk, HEAD_DIM), lambda b, h, i: (b, 0, h)),
            pl.BlockSpec((1, bq, gw), lambda b, h, i: (b, i, goff + h)),
        ],
        out_specs=pl.BlockSpec((1, bq, gw), lambda b, h, i: (b, i, h)),
        out_shape=jax.ShapeDtypeStruct((bsz, s, aw), BF16),
        compiler_params=_cparams(("arbitrary", "arbitrary", "arbitrary")),
        name="gqa_attention",
    )(q, k_all, v_all, proj)


def _rope_tables(n_tokens):
    rows = n_tokens // GRID_W
    row = jnp.repeat(jnp.arange(rows, dtype=F32), GRID_W)
    col = jnp.tile(jnp.arange(GRID_W, dtype=F32), rows)
    axis_dim = HEAD_DIM // 2
    inv = 1.0 / (ROPE_THETA ** (jnp.arange(0, axis_dim, 2, dtype=F32) / axis_dim))
    ang = jnp.concatenate([row[:, None] * inv, col[:, None] * inv], axis=-1)
    cosf = jnp.repeat(jnp.cos(ang), 2, axis=-1)
    sinf = jnp.repeat(jnp.sin(ang), 2, axis=-1) * jnp.tile(jnp.array([-1.0, 1.0], F32), HEAD_DIM // 2)
    return cosf, sinf


def kernel(x, c, ctx, c_ctx, mod_w, mod_b, post_g, post_b, a_w_in, a_ln_g, a_ln_b, a_w_s, a_b_s, a_w_out, b_w_in, b_conv_w, b_conv_b, b_ln_g, b_ln_b, b_w_out, c_w_in, c_q_g, c_k_g, c_w_out):
    bsz, n_tokens, d = x.shape
    lc = ctx.shape[1]
    assert bsz + 1 <= MOD_ROWS
    kinds = [i % N_MIXERS for i in range(DEPTH)]

    cond = jnp.zeros((MOD_ROWS, d), F32).at[:bsz].set(c).at[bsz].set(c_ctx)
    mods = _modulation(cond, mod_w, mod_b)

    def mod_parts(i, ctx_rows):
        m = mods[i, bsz:bsz + 1] if ctx_rows else mods[i, :bsz]
        m = jnp.broadcast_to(m, (bsz, 3 * d)).reshape(bsz, 1, 3 * d)
        return m[..., :d], m[..., d:2 * d], m[..., 2 * d:]

    aw = N_HEADS * HEAD_DIM
    kw = N_KV_HEADS * HEAD_DIM
    for i in range(DEPTH):
        kind = kinds[i]
        slot = kinds[:i].count(kind)
        ctx_read_later = 2 in kinds[i + 1:]
        streams = [(x, False)] + ([(ctx, True)] if ctx_read_later else [])
        outs = []
        if kind == 0:
            w_in = a_w_in[slot].astype(BF16)
            w_out = a_w_out[slot].astype(BF16)
            w_s = a_w_s[slot].astype(BF16)
            b_s_t = a_b_s[slot].T
            for t, is_ctx in streams:
                shift, scale, gate = mod_parts(i, is_ctx)
                uvg = _mod_matmul(t, scale, shift, w_in)
                z = _sgu_gate(uvg, a_ln_g[slot], a_ln_b[slot], w_s, b_s_t)
                outs.append(_out_matmul_norm(z, w_out, t, gate, post_g[i], post_b[i]))
        elif kind == 1:
            w_in = b_w_in[slot].astype(BF16)
            w_out = b_w_out[slot].astype(BF16)
            for t, is_ctx in streams:
                shift, scale, gate = mod_parts(i, is_ctx)
                abg = _mod_matmul(t, scale, shift, w_in)
                z = _conv_gate(abg, b_conv_w[slot], b_conv_b[slot], b_ln_g[slot], b_ln_b[slot])
                outs.append(_out_matmul_norm(z, w_out, t, gate, post_g[i], post_b[i]))
        else:
            assert not ctx_read_later, "no later layer reads context after the attention layer at this depth"
            w_in = c_w_in[slot].astype(BF16)
            w_out = c_w_out[slot].astype(BF16)
            shift, scale, gate = mod_parts(i, False)
            shift_c, scale_c, _ = mod_parts(i, True)
            proj = _mod_matmul(x, scale, shift, w_in)
            proj_c = _mod_matmul(ctx, scale_c, shift_c, w_in[:, aw:aw + 2 * kw])
            cosf, sinf = _rope_tables(n_tokens)
            q, k, v = _qk_norm_rope(proj, cosf, sinf, c_q_g[slot], c_k_g[slot], True)
            kc, vc = _qk_norm_rope(proj_c, jnp.ones((lc, HEAD_DIM), F32), jnp.zeros((lc, HEAD_DIM), F32),
                                   c_q_g[slot], c_k_g[slot], False)
            k_all = jnp.concatenate([k, kc], axis=1)
            v_all = jnp.concatenate([v, vc], axis=1)
            z = _attention(q, k_all, v_all, proj)
            outs.append(_out_matmul_norm(z, w_out, x, gate, post_g[i], post_b[i]))
        x = outs[0]
        if ctx_read_later:
            ctx = outs[1]
    return x
```

```python
import functools

import jax
import jax.numpy as jnp
from jax import lax
from jax.experimental import pallas as pl
from jax.experimental.pallas import tpu as pltpu

F32 = jnp.float32
BF16 = jnp.bfloat16

DEPTH = 4
N_MIXERS = 3
GRID_W = 64
CHUNK = 128
SGU_GROUPS = 16
CONV_W = 31
CONV_PAD = CONV_W // 2
HALO = 16
HEAD_DIM = 128
N_HEADS = 16
N_KV_HEADS = 4
GQA_GROUP = N_HEADS // N_KV_HEADS
ROPE_THETA = 10000.0
DEEPNORM_ALPHA = (2 * DEPTH) ** 0.25
LN_EPS = 1e-6
MOD_ROWS = 8
NORM_ROWS = 32

VMEM_LIMIT = 56 * 1024 * 1024


def _cparams(sem):
    return pltpu.CompilerParams(dimension_semantics=sem, vmem_limit_bytes=VMEM_LIMIT)


def _layer_norm(x, g, b):
    mu = jnp.mean(x, axis=-1, keepdims=True)
    xc = x - mu
    var = jnp.mean(xc * xc, axis=-1, keepdims=True)
    return xc * lax.rsqrt(var + LN_EPS) * g + b


def _silu(x):
    return x * jax.nn.sigmoid(x)


def _mod_kernel(c_ref, w_ref, b_ref, o_ref):
    s = _silu(c_ref[...]).astype(BF16)
    o_ref[0] = jnp.dot(s, w_ref[0].astype(BF16), preferred_element_type=F32) + b_ref[0]


def _modulation(cond, mod_w, mod_b):
    depth, d, n = mod_w.shape
    bn = 1024
    return pl.pallas_call(
        _mod_kernel,
        grid=(depth, n // bn),
        in_specs=[
            pl.BlockSpec((MOD_ROWS, d), lambda i, j: (0, 0)),
            pl.BlockSpec((1, d, bn), lambda i, j: (i, 0, j)),
            pl.BlockSpec((1, 1, bn), lambda i, j: (i, 0, j)),
        ],
        out_specs=pl.BlockSpec((1, MOD_ROWS, bn), lambda i, j: (i, 0, j)),
        out_shape=jax.ShapeDtypeStruct((depth, MOD_ROWS, n), F32),
        compiler_params=_cparams(("arbitrary", "arbitrary")),
        name="modulation",
    )(cond, mod_w, mod_b.reshape(depth, 1, n))


def _mm_kernel(x_ref, sc_ref, sh_ref, w_ref, o_ref, h_ref):
    @pl.when(pl.program_id(2) == 0)
    def _():
        h_ref[...] = (x_ref[0] * (1.0 + sc_ref[0]) + sh_ref[0]).astype(BF16)

    o_ref[0] = jnp.dot(h_ref[...], w_ref[...], preferred_element_type=F32).astype(o_ref.dtype)


def _mod_matmul(x, scale, shift, w, bm=1024, bn=1024):
    bsz, s, d = x.shape
    n = w.shape[1]
    bm = min(bm, s)
    bn = min(bn, n)
    return pl.pallas_call(
        _mm_kernel,
        grid=(bsz, s // bm, n // bn),
        in_specs=[
            pl.BlockSpec((1, bm, d), lambda b, i, j: (b, i, 0)),
            pl.BlockSpec((1, 1, d), lambda b, i, j: (b, 0, 0)),
            pl.BlockSpec((1, 1, d), lambda b, i, j: (b, 0, 0)),
            pl.BlockSpec((d, bn), lambda b, i, j: (0, j)),
        ],
        out_specs=pl.BlockSpec((1, bm, bn), lambda b, i, j: (b, i, j)),
        out_shape=jax.ShapeDtypeStruct((bsz, s, n), BF16),
        scratch_shapes=[pltpu.VMEM((bm, d), BF16)],
        compiler_params=_cparams(("arbitrary", "arbitrary", "arbitrary")),
        name="mod_matmul",
    )(x, scale, shift, w)


def _out_kernel(z_ref, w_ref, x_ref, gate_ref, pg_ref, pb_ref, o_ref):
    k = pl.program_id(2)
    part = jnp.dot(z_ref[0], w_ref[...], preferred_element_type=F32)

    @pl.when(k == 0)
    def _():
        o_ref[0] = part

    @pl.when(k > 0)
    def _():
        o_ref[0] += part

    @pl.when(k == pl.num_programs(2) - 1)
    def _():
        def norm_tile(t, carry):
            rows = pl.ds(pl.multiple_of(t * NORM_ROWS, NORM_ROWS), NORM_ROWS)
            r = DEEPNORM_ALPHA * x_ref[0, rows, :] + gate_ref[0] * o_ref[0, rows, :]
            o_ref[0, rows, :] = _layer_norm(r, pg_ref[...], pb_ref[...])
            return carry

        lax.fori_loop(0, o_ref.shape[1] // NORM_ROWS, norm_tile, 0)


def _out_matmul_norm(z, w, x, gate, post_g, post_b, bm=1024, bk=512):
    bsz, s, kdim = z.shape
    d = w.shape[1]
    bm = min(bm, s)
    bk = min(bk, kdim)
    return pl.pallas_call(
        _out_kernel,
        grid=(bsz, s // bm, kdim // bk),
        in_specs=[
            pl.BlockSpec((1, bm, bk), lambda b, i, k: (b, i, k)),
            pl.BlockSpec((bk, d), lambda b, i, k: (k, 0)),
            pl.BlockSpec((1, bm, d), lambda b, i, k: (b, i, 0)),
            pl.BlockSpec((1, 1, d), lambda b, i, k: (b, 0, 0)),
            pl.BlockSpec((1, d), lambda b, i, k: (0, 0)),
            pl.BlockSpec((1, d), lambda b, i, k: (0, 0)),
        ],
        out_specs=pl.BlockSpec((1, bm, d), lambda b, i, k: (b, i, 0)),
        out_shape=jax.ShapeDtypeStruct((bsz, s, d), F32),
        compiler_params=_cparams(("arbitrary", "arbitrary", "arbitrary")),
        name="out_matmul_norm",
    )(z, w, x, gate, post_g.reshape(1, d), post_b.reshape(1, d))


def _sgu_kernel(u_ref, v_ref, g_ref, lg_ref, lb_ref, ws_ref, bs_ref, z_ref, vn_ref):
    bm = v_ref.shape[1]
    gw = v_ref.shape[2] // SGU_GROUPS
    for c in range(bm // CHUNK):
        rows = slice(c * CHUNK, (c + 1) * CHUNK)
        vn_ref[...] = _layer_norm(v_ref[0, rows, :].astype(F32), lg_ref[...], lb_ref[...]).astype(BF16)
        for h in range(SGU_GROUPS):
            cols = slice(h * gw, (h + 1) * gw)
            s = jnp.dot(ws_ref[h], vn_ref[:, cols], preferred_element_type=F32) + bs_ref[:, h:h + 1]
            u = u_ref[0, rows, cols].astype(F32)
            z_ref[0, rows, cols] = (u * s * _silu(g_ref[0, rows, cols].astype(F32))).astype(BF16)


def _sgu_gate(uvg, ln_g, ln_b, w_s, b_s_t, bm=512):
    bsz, s, n3 = uvg.shape
    e = n3 // 3
    bm = min(bm, s)
    return pl.pallas_call(
        _sgu_kernel,
        grid=(bsz, s // bm),
        in_specs=[
            pl.BlockSpec((1, bm, e), lambda b, i: (b, i, 0)),
            pl.BlockSpec((1, bm, e), lambda b, i: (b, i, 1)),
            pl.BlockSpec((1, bm, e), lambda b, i: (b, i, 2)),
            pl.BlockSpec((1, e), lambda b, i: (0, 0)),
            pl.BlockSpec((1, e), lambda b, i: (0, 0)),
            pl.BlockSpec((SGU_GROUPS, CHUNK, CHUNK), lambda b, i: (0, 0, 0)),
            pl.BlockSpec((CHUNK, SGU_GROUPS), lambda b, i: (0, 0)),
        ],
        out_specs=pl.BlockSpec((1, bm, e), lambda b, i: (b, i, 0)),
        out_shape=jax.ShapeDtypeStruct((bsz, s, e), BF16),
        scratch_shapes=[pltpu.VMEM((CHUNK, e), BF16)],
        compiler_params=_cparams(("arbitrary", "arbitrary")),
        name="sgu_gate",
    )(uvg, uvg, uvg, ln_g.reshape(1, e), ln_b.reshape(1, e), w_s, b_s_t)


CONV_ROWS = 128
CONV_COLS = 128
SUBLANES = 8


def _glu(a, b):
    return a.astype(F32) * jax.nn.sigmoid(b.astype(F32))


def _conv_kernel(a_ref, b_ref, g_ref, ap_ref, bp_ref, an_ref, bn_ref,
                 cw_ref, cb_ref, lg_ref, lb_ref, z_ref, yext_ref, yc_ref):
    i = pl.program_id(1)
    bm = a_ref.shape[1]
    e = a_ref.shape[2]
    yext_ref[HALO:HALO + bm, :] = _glu(a_ref[0], b_ref[0])
    yext_ref[0:HALO, :] = jnp.where(i > 0, _glu(ap_ref[0], bp_ref[0]), 0.0)
    yext_ref[HALO + bm:HALO + bm + HALO, :] = jnp.where(
        i < pl.num_programs(1) - 1, _glu(an_ref[0], bn_ref[0]), 0.0)

    assert HALO - CONV_PAD == 1
    rc = min(CONV_ROWS, bm)
    for r0 in range(0, bm, rc):
        for c0 in range(0, e, CONV_COLS):
            cols = slice(c0, c0 + CONV_COLS)
            acc = jnp.broadcast_to(cb_ref[:, cols], (rc, CONV_COLS))
            for r in range(SUBLANES):
                part = None
                for j in range(r if r else SUBLANES, CONV_W + 1, SUBLANES):
                    lo = r0 + j - r
                    term = cw_ref[j - 1:j, cols] * yext_ref[lo:lo + rc + SUBLANES, cols]
                    part = term if part is None else part + term
                acc = acc + part[r:r + rc]
            yc_ref[r0:r0 + rc, cols] = acc

    def norm_tile(t, carry):
        rows = pl.ds(pl.multiple_of(t * NORM_ROWS, NORM_ROWS), NORM_ROWS)
        y = _silu(_layer_norm(yc_ref[rows, :], lg_ref[...], lb_ref[...]))
        z_ref[0, rows, :] = (y * _silu(g_ref[0, rows, :].astype(F32))).astype(BF16)
        return carry

    lax.fori_loop(0, bm // NORM_ROWS, norm_tile, 0)


def _conv_gate(abg, conv_w, conv_b, ln_g, ln_b, bm=256):
    bsz, s, n3 = abg.shape
    e = n3 // 3
    bm = min(bm, s)
    hb = bm // HALO
    last = s // HALO - 1

    def prev_map(col):
        return lambda b, i: (b, jnp.maximum(i * hb - 1, 0), col)

    def next_map(col):
        return lambda b, i: (b, jnp.minimum((i + 1) * hb, last), col)

    return pl.pallas_call(
        _conv_kernel,
        grid=(bsz, s // bm),
        in_specs=[
            pl.BlockSpec((1, bm, e), lambda b, i: (b, i, 0)),
            pl.BlockSpec((1, bm, e), lambda b, i: (b, i, 1)),
            pl.BlockSpec((1, bm, e), lambda b, i: (b, i, 2)),
            pl.BlockSpec((1, HALO, e), prev_map(0)),
            pl.BlockSpec((1, HALO, e), prev_map(1)),
            pl.BlockSpec((1, HALO, e), next_map(0)),
            pl.BlockSpec((1, HALO, e), next_map(1)),
            pl.BlockSpec((CONV_W, e), lambda b, i: (0, 0)),
            pl.BlockSpec((1, e), lambda b, i: (0, 0)),
            pl.BlockSpec((1, e), lambda b, i: (0, 0)),
            pl.BlockSpec((1, e), lambda b, i: (0, 0)),
        ],
        out_specs=pl.BlockSpec((1, bm, e), lambda b, i: (b, i, 0)),
        out_shape=jax.ShapeDtypeStruct((bsz, s, e), BF16),
        scratch_shapes=[pltpu.VMEM((bm + 2 * HALO, e), F32), pltpu.VMEM((bm, e), F32)],
        compiler_params=_cparams(("arbitrary", "arbitrary")),
        name="conv_gate",
    )(abg, abg, abg, abg, abg, abg, abg, conv_w, conv_b.reshape(1, e),
      ln_g.reshape(1, e), ln_b.reshape(1, e))


def _norm_rope_head(t, gain, cosf, sinf, even):
    ms = jnp.mean(t * t, axis=-1, keepdims=True)
    tn = t * lax.rsqrt(ms + LN_EPS) * gain
    swapped = jnp.where(even, pltpu.roll(tn, HEAD_DIM - 1, 1), pltpu.roll(tn, 1, 1))
    return tn * cosf + swapped * sinf


def _qk_kernel(nq, *refs):
    if nq:
        q_ref, k_ref, v_ref, cos_ref, sin_ref, qg_ref, kg_ref, qo_ref, ko_ref, vo_ref = refs
    else:
        k_ref, v_ref, cos_ref, sin_ref, qg_ref, kg_ref, ko_ref, vo_ref = refs
    cosf = cos_ref[...]
    sinf = sin_ref[...]
    even = (lax.broadcasted_iota(jnp.int32, cosf.shape, 1) % 2) == 0
    for h in range(nq):
        cols = slice(h * HEAD_DIM, (h + 1) * HEAD_DIM)
        qh = _norm_rope_head(q_ref[0, :, cols].astype(F32), qg_ref[...], cosf, sinf, even)
        qo_ref[0, :, cols] = (qh * (HEAD_DIM ** -0.5)).astype(BF16)
    for h in range(N_KV_HEADS):
        cols = slice(h * HEAD_DIM, (h + 1) * HEAD_DIM)
        kh = _norm_rope_head(k_ref[0, :, cols].astype(F32), kg_ref[...], cosf, sinf, even)
        ko_ref[0, :, cols] = kh.astype(BF16)
    vo_ref[0] = v_ref[0]


def _qk_norm_rope(proj, cosf, sinf, q_g, k_g, with_q, bm=256):
    bsz, s, _ = proj.shape
    bm = min(bm, s)
    aw = N_HEADS * HEAD_DIM
    kw = N_KV_HEADS * HEAD_DIM
    koff = (aw // kw) if with_q else 0
    in_specs = []
    out_specs = []
    out_shape = []
    args = []
    if with_q:
        in_specs.append(pl.BlockSpec((1, bm, aw), lambda b, i: (b, i, 0)))
        args.append(proj)
        out_specs.append(pl.BlockSpec((1, bm, aw), lambda b, i: (b, i, 0)))
        out_shape.append(jax.ShapeDtypeStruct((bsz, s, aw), BF16))
    in_specs += [
        pl.BlockSpec((1, bm, kw), lambda b, i: (b, i, koff)),
        pl.BlockSpec((1, bm, kw), lambda b, i: (b, i, koff + 1)),
        pl.BlockSpec((bm, HEAD_DIM), lambda b, i: (i, 0)),
        pl.BlockSpec((bm, HEAD_DIM), lambda b, i: (i, 0)),
        pl.BlockSpec((1, HEAD_DIM), lambda b, i: (0, 0)),
        pl.BlockSpec((1, HEAD_DIM), lambda b, i: (0, 0)),
    ]
    args += [proj, proj, cosf, sinf, q_g.reshape(1, HEAD_DIM), k_g.reshape(1, HEAD_DIM)]
    out_specs += [pl.BlockSpec((1, bm, kw), lambda b, i: (b, i, 0))] * 2
    out_shape += [jax.ShapeDtypeStruct((bsz, s, kw), BF16)] * 2
    return pl.pallas_call(
        functools.partial(_qk_kernel, N_HEADS if with_q else 0),
        grid=(bsz, s // bm),
        in_specs=in_specs,
        out_specs=out_specs,
        out_shape=out_shape,
        compiler_params=_cparams(("arbitrary", "arbitrary")),
        name="qk_norm_rope",
    )(*args)


def _attn_kernel(q_ref, k_ref, v_ref, g_ref, z_ref):
    k = k_ref[0]
    v = v_ref[0]
    for h in range(GQA_GROUP):
        cols = slice(h * HEAD_DIM, (h + 1) * HEAD_DIM)
        s = lax.dot_general(q_ref[0, :, cols], k, (((1,), (1,)), ((), ())), preferred_element_type=F32)
        m = jnp.max(s, axis=-1, keepdims=True)
        p = jnp.exp(s - m)
        l = jnp.sum(p, axis=-1, keepdims=True)
        o = jnp.dot(p.astype(BF16), v, preferred_element_type=F32) / l
        z_ref[0, :, cols] = (o * _silu(g_ref[0, :, cols].astype(F32))).astype(BF16)


def _attention(q, k_all, v_all, proj, bq=256):
    bsz, s, aw = q.shape
    lk = k_all.shape[1]
    gw = GQA_GROUP * HEAD_DIM
    goff = (proj.shape[2] - aw) // gw
    bq = min(bq, s)
    return pl.pallas_call(
        _attn_kernel,
        grid=(bsz, N_KV_HEADS, s // bq),
        in_specs=[
            pl.BlockSpec((1, bq, gw), lambda b, h, i: (b, i, h)),
            pl.BlockSpec((1, lk, HEAD_DIM), lambda b, h, i: (b, 0, h)),
            pl.BlockSpec((1, lk, HEAD_DIM), lambda b, h, i: (b, 0, h)),
            pl.BlockSpec((1, bq, gw), lambda b, h, i: (b, i, goff + h)),
        ],
        out_specs=pl.BlockSpec((1, bq, gw), lambda b, h, i: (b, i, h)),
        out_shape=jax.ShapeDtypeStruct((bsz, s, aw), BF16),
        compiler_params=_cparams(("arbitrary", "arbitrary", "arbitrary")),
        name="gqa_attention",
    )(q, k_all, v_all, proj)


def _rope_tables(n_tokens):
    rows = n_tokens // GRID_W
    row = jnp.repeat(jnp.arange(rows, dtype=F32), GRID_W)
    col = jnp.tile(jnp.arange(GRID_W, dtype=F32), rows)
    axis_dim = HEAD_DIM // 2
    inv = 1.0 / (ROPE_THETA ** (jnp.arange(0, axis_dim, 2, dtype=F32) / axis_dim))
    ang = jnp.concatenate([row[:, None] * inv, col[:, None] * inv], axis=-1)
    cosf = jnp.repeat(jnp.cos(ang), 2, axis=-1)
    sinf = jnp.repeat(jnp.sin(ang), 2, axis=-1) * jnp.tile(jnp.array([-1.0, 1.0], F32), HEAD_DIM // 2)
    return cosf, sinf


def kernel(x, c, ctx, c_ctx, mod_w, mod_b, post_g, post_b, a_w_in, a_ln_g, a_ln_b, a_w_s, a_b_s, a_w_out, b_w_in, b_conv_w, b_conv_b, b_ln_g, b_ln_b, b_w_out, c_w_in, c_q_g, c_k_g, c_w_out):
    bsz, n_tokens, d = x.shape
    lc = ctx.shape[1]
    assert bsz + 1 <= MOD_ROWS
    kinds = [i % N_MIXERS for i in range(DEPTH)]

    cond = jnp.zeros((MOD_ROWS, d), F32).at[:bsz].set(c).at[bsz].set(c_ctx)
    mods = _modulation(cond, mod_w, mod_b)

    def mod_parts(i, ctx_rows):
        m = mods[i, bsz:bsz + 1] if ctx_rows else mods[i, :bsz]
        m = jnp.broadcast_to(m, (bsz, 3 * d)).reshape(bsz, 1, 3 * d)
        return m[..., :d], m[..., d:2 * d], m[..., 2 * d:]

    aw = N_HEADS * HEAD_DIM
    kw = N_KV_HEADS * HEAD_DIM
    for i in range(DEPTH):
        kind = kinds[i]
        slot = kinds[:i].count(kind)
        ctx_read_later = 2 in kinds[i + 1:]
        streams = [(x, False)] + ([(ctx, True)] if ctx_read_later else [])
        outs = []
        if kind == 0:
            w_in = a_w_in[slot].astype(BF16)
            w_out = a_w_out[slot].astype(BF16)
            w_s = a_w_s[slot].astype(BF16)
            b_s_t = a_b_s[slot].T
            for t, is_ctx in streams:
                shift, scale, gate = mod_parts(i, is_ctx)
                uvg = _mod_matmul(t, scale, shift, w_in)
                z = _sgu_gate(uvg, a_ln_g[slot], a_ln_b[slot], w_s, b_s_t)
                outs.append(_out_matmul_norm(z, w_out, t, gate, post_g[i], post_b[i]))
        elif kind == 1:
            w_in = b_w_in[slot].astype(BF16)
            w_out = b_w_out[slot].astype(BF16)
            for t, is_ctx in streams:
                shift, scale, gate = mod_parts(i, is_ctx)
                abg = _mod_matmul(t, scale, shift, w_in)
                z = _conv_gate(abg, b_conv_w[slot], b_conv_b[slot], b_ln_g[slot], b_ln_b[slot])
                outs.append(_out_matmul_norm(z, w_out, t, gate, post_g[i], post_b[i]))
        else:
            assert not ctx_read_later, "no later layer reads context after the attention layer at this depth"
            w_in = c_w_in[slot].astype(BF16)
            w_out = c_w_out[slot].astype(BF16)
            shift, scale, gate = mod_parts(i, False)
            shift_c, scale_c, _ = mod_parts(i, True)
            proj = _mod_matmul(x, scale, shift, w_in)
            proj_c = _mod_matmul(ctx, scale_c, shift_c, w_in[:, aw:aw + 2 * kw])
            cosf, sinf = _rope_tables(n_tokens)
            q, k, v = _qk_norm_rope(proj, cosf, sinf, c_q_g[slot], c_k_g[slot], True)
            kc, vc = _qk_norm_rope(proj_c, jnp.ones((lc, HEAD_DIM), F32), jnp.zeros((lc, HEAD_DIM), F32),
                                   c_q_g[slot], c_k_g[slot], False)
            k_all = jnp.concatenate([k, kc], axis=1)
            v_all = jnp.concatenate([v, vc], axis=1)
            z = _attention(q, k_all, v_all, proj)
            outs.append(_out_matmul_norm(z, w_out, x, gate, post_g[i], post_b[i]))
        x = outs[0]
        if ctx_read_later:
            ctx = outs[1]
    return x
```

```python
import functools

import jax
import jax.numpy as jnp
from jax import lax
from jax.experimental import pallas as pl
from jax.experimental.pallas import tpu as pltpu

F32 = jnp.float32
BF16 = jnp.bfloat16

DEPTH = 4
N_MIXERS = 3
GRID_W = 64
CHUNK = 128
SGU_GROUPS = 16
CONV_W = 31
CONV_PAD = CONV_W // 2
HALO = 16
HEAD_DIM = 128
N_HEADS = 16
N_KV_HEADS = 4
GQA_GROUP = N_HEADS // N_KV_HEADS
ROPE_THETA = 10000.0
Q_SCALE = HEAD_DIM ** -0.5 * 1.4426950408889634
DEEPNORM_ALPHA = (2 * DEPTH) ** 0.25
LN_EPS = 1e-6
MOD_ROWS = 8
NORM_ROWS = 32

VMEM_LIMIT = 56 * 1024 * 1024


def _cparams(sem):
    return pltpu.CompilerParams(dimension_semantics=sem, vmem_limit_bytes=VMEM_LIMIT)


def _layer_norm(x, g, b):
    mu = jnp.mean(x, axis=-1, keepdims=True)
    xc = x - mu
    var = jnp.mean(xc * xc, axis=-1, keepdims=True)
    return xc * lax.rsqrt(var + LN_EPS) * g + b


def _silu(x):
    return x * jax.nn.sigmoid(x)


def _mod_kernel(c_ref, w_ref, b_ref, o_ref):
    s = _silu(c_ref[...]).astype(BF16)
    o_ref[0] = jnp.dot(s, w_ref[0].astype(BF16), preferred_element_type=F32) + b_ref[0]


def _modulation(cond, mod_w, mod_b):
    depth, d, n = mod_w.shape
    bn = 1024
    return pl.pallas_call(
        _mod_kernel,
        grid=(depth, n // bn),
        in_specs=[
            pl.BlockSpec((MOD_ROWS, d), lambda i, j: (0, 0)),
            pl.BlockSpec((1, d, bn), lambda i, j: (i, 0, j)),
            pl.BlockSpec((1, 1, bn), lambda i, j: (i, 0, j)),
        ],
        out_specs=pl.BlockSpec((1, MOD_ROWS, bn), lambda i, j: (i, 0, j)),
        out_shape=jax.ShapeDtypeStruct((depth, MOD_ROWS, n), F32),
        compiler_params=_cparams(("arbitrary", "arbitrary")),
        name="modulation",
    )(cond, mod_w, mod_b.reshape(depth, 1, n))


def _mm_kernel(x_ref, sc_ref, sh_ref, w_ref, o_ref, h_ref):
    @pl.when(pl.program_id(2) == 0)
    def _():
        h_ref[...] = (x_ref[0] * (1.0 + sc_ref[0]) + sh_ref[0]).astype(BF16)

    o_ref[0] = jnp.dot(h_ref[...], w_ref[...], preferred_element_type=F32).astype(o_ref.dtype)


def _mod_matmul(x, scale, shift, w, bm=1024, bn=1024):
    bsz, s, d = x.shape
    n = w.shape[1]
    bm = min(bm, s)
    bn = min(bn, n)
    return pl.pallas_call(
        _mm_kernel,
        grid=(bsz, s // bm, n // bn),
        in_specs=[
            pl.BlockSpec((1, bm, d), lambda b, i, j: (b, i, 0)),
            pl.BlockSpec((1, 1, d), lambda b, i, j: (b, 0, 0)),
            pl.BlockSpec((1, 1, d), lambda b, i, j: (b, 0, 0)),
            pl.BlockSpec((d, bn), lambda b, i, j: (0, j)),
        ],
        out_specs=pl.BlockSpec((1, bm, bn), lambda b, i, j: (b, i, j)),
        out_shape=jax.ShapeDtypeStruct((bsz, s, n), BF16),
        scratch_shapes=[pltpu.VMEM((bm, d), BF16)],
        compiler_params=_cparams(("arbitrary", "arbitrary", "arbitrary")),
        name="mod_matmul",
    )(x, scale, shift, w)


OUT_SUB_ROWS = 256


def _out_kernel(z_ref, w_ref, x_ref, gate_ref, pg_ref, pb_ref, o_ref):
    bm = o_ref.shape[1]
    sub = min(OUT_SUB_ROWS, bm)
    for r0 in range(0, bm, sub):
        rows = slice(r0, r0 + sub)
        y = jnp.dot(z_ref[0, rows, :], w_ref[...], preferred_element_type=F32)
        o_ref[0, rows, :] = DEEPNORM_ALPHA * x_ref[0, rows, :] + gate_ref[0] * y
        for t0 in range(r0, r0 + sub, NORM_ROWS):
            tile = slice(t0, t0 + NORM_ROWS)
            o_ref[0, tile, :] = _layer_norm(o_ref[0, tile, :], pg_ref[...], pb_ref[...])


def _out_matmul_norm(z, w, x, gate, post_g, post_b, bm=512):
    bsz, s, kdim = z.shape
    d = w.shape[1]
    bm = min(bm, s)
    return pl.pallas_call(
        _out_kernel,
        grid=(bsz, s // bm),
        in_specs=[
            pl.BlockSpec((1, bm, kdim), lambda b, i: (b, i, 0)),
            pl.BlockSpec((kdim, d), lambda b, i: (0, 0), pipeline_mode=pl.Buffered(1)),
            pl.BlockSpec((1, bm, d), lambda b, i: (b, i, 0)),
            pl.BlockSpec((1, 1, d), lambda b, i: (b, 0, 0)),
            pl.BlockSpec((1, d), lambda b, i: (0, 0)),
            pl.BlockSpec((1, d), lambda b, i: (0, 0)),
        ],
        out_specs=pl.BlockSpec((1, bm, d), lambda b, i: (b, i, 0)),
        out_shape=jax.ShapeDtypeStruct((bsz, s, d), F32),
        compiler_params=_cparams(("arbitrary", "arbitrary")),
        name="out_matmul_norm",
    )(z, w, x, gate, post_g.reshape(1, d), post_b.reshape(1, d))


def _sgu_kernel(u_ref, v_ref, g_ref, lg_ref, lb_ref, ws_ref, bs_ref, z_ref, vn_ref):
    bm = v_ref.shape[1]
    gw = v_ref.shape[2] // SGU_GROUPS
    for c in range(bm // CHUNK):
        rows = slice(c * CHUNK, (c + 1) * CHUNK)
        vn_ref[...] = _layer_norm(v_ref[0, rows, :].astype(F32), lg_ref[...], lb_ref[...]).astype(BF16)
        for h in range(SGU_GROUPS):
            cols = slice(h * gw, (h + 1) * gw)
            s = jnp.dot(ws_ref[h], vn_ref[:, cols], preferred_element_type=F32) + bs_ref[:, h:h + 1]
            u = u_ref[0, rows, cols].astype(F32)
            z_ref[0, rows, cols] = (u * s * _silu(g_ref[0, rows, cols].astype(F32))).astype(BF16)


def _sgu_gate(uvg, ln_g, ln_b, w_s, b_s_t, bm=512):
    bsz, s, n3 = uvg.shape
    e = n3 // 3
    bm = min(bm, s)
    return pl.pallas_call(
        _sgu_kernel,
        grid=(bsz, s // bm),
        in_specs=[
            pl.BlockSpec((1, bm, e), lambda b, i: (b, i, 0)),
            pl.BlockSpec((1, bm, e), lambda b, i: (b, i, 1)),
            pl.BlockSpec((1, bm, e), lambda b, i: (b, i, 2)),
            pl.BlockSpec((1, e), lambda b, i: (0, 0)),
            pl.BlockSpec((1, e), lambda b, i: (0, 0)),
            pl.BlockSpec((SGU_GROUPS, CHUNK, CHUNK), lambda b, i: (0, 0, 0)),
            pl.BlockSpec((CHUNK, SGU_GROUPS), lambda b, i: (0, 0)),
        ],
        out_specs=pl.BlockSpec((1, bm, e), lambda b, i: (b, i, 0)),
        out_shape=jax.ShapeDtypeStruct((bsz, s, e), BF16),
        scratch_shapes=[pltpu.VMEM((CHUNK, e), BF16)],
        compiler_params=_cparams(("arbitrary", "arbitrary")),
        name="sgu_gate",
    )(uvg, uvg, uvg, ln_g.reshape(1, e), ln_b.reshape(1, e), w_s, b_s_t)


CONV_ROWS = 128
CONV_COLS = 128
SUBLANES = 8


def _glu(a, b):
    return a.astype(F32) * jax.nn.sigmoid(b.astype(F32))


def _conv_kernel(a_ref, b_ref, g_ref, ap_ref, bp_ref, an_ref, bn_ref,
                 cw_ref, cb_ref, lg_ref, lb_ref, z_ref, yext_ref, yc_ref):
    i = pl.program_id(1)
    bm = a_ref.shape[1]
    e = a_ref.shape[2]
    yext_ref[HALO:HALO + bm, :] = _glu(a_ref[0], b_ref[0])
    yext_ref[0:HALO, :] = jnp.where(i > 0, _glu(ap_ref[0], bp_ref[0]), 0.0)
    yext_ref[HALO + bm:HALO + bm + HALO, :] = jnp.where(
        i < pl.num_programs(1) - 1, _glu(an_ref[0], bn_ref[0]), 0.0)

    assert HALO - CONV_PAD == 1
    rc = min(CONV_ROWS, bm)
    for r0 in range(0, bm, rc):
        for c0 in range(0, e, CONV_COLS):
            cols = slice(c0, c0 + CONV_COLS)
            acc = jnp.broadcast_to(cb_ref[:, cols], (rc, CONV_COLS))
            for r in range(SUBLANES):
                part = None
                for j in range(r if r else SUBLANES, CONV_W + 1, SUBLANES):
                    lo = r0 + j - r
                    term = cw_ref[j - 1:j, cols] * yext_ref[lo:lo + rc + SUBLANES, cols]
                    part = term if part is None else part + term
                acc = acc + part[r:r + rc]
            yc_ref[r0:r0 + rc, cols] = acc

    def norm_tile(t, carry):
        rows = pl.ds(pl.multiple_of(t * NORM_ROWS, NORM_ROWS), NORM_ROWS)
        y = _silu(_layer_norm(yc_ref[rows, :], lg_ref[...], lb_ref[...]))
        z_ref[0, rows, :] = (y * _silu(g_ref[0, rows, :].astype(F32))).astype(BF16)
        return carry

    lax.fori_loop(0, bm // NORM_ROWS, norm_tile, 0)


def _conv_gate(abg, conv_w, conv_b, ln_g, ln_b, bm=256):
    bsz, s, n3 = abg.shape
    e = n3 // 3
    bm = min(bm, s)
    hb = bm // HALO
    last = s // HALO - 1

    def prev_map(col):
        return lambda b, i: (b, jnp.maximum(i * hb - 1, 0), col)

    def next_map(col):
        return lambda b, i: (b, jnp.minimum((i + 1) * hb, last), col)

    return pl.pallas_call(
        _conv_kernel,
        grid=(bsz, s // bm),
        in_specs=[
            pl.BlockSpec((1, bm, e), lambda b, i: (b, i, 0)),
            pl.BlockSpec((1, bm, e), lambda b, i: (b, i, 1)),
            pl.BlockSpec((1, bm, e), lambda b, i: (b, i, 2)),
            pl.BlockSpec((1, HALO, e), prev_map(0)),
            pl.BlockSpec((1, HALO, e), prev_map(1)),
            pl.BlockSpec((1, HALO, e), next_map(0)),
            pl.BlockSpec((1, HALO, e), next_map(1)),
            pl.BlockSpec((CONV_W, e), lambda b, i: (0, 0)),
            pl.BlockSpec((1, e), lambda b, i: (0, 0)),
            pl.BlockSpec((1, e), lambda b, i: (0, 0)),
            pl.BlockSpec((1, e), lambda b, i: (0, 0)),
        ],
        out_specs=pl.BlockSpec((1, bm, e), lambda b, i: (b, i, 0)),
        out_shape=jax.ShapeDtypeStruct((bsz, s, e), BF16),
        scratch_shapes=[pltpu.VMEM((bm + 2 * HALO, e), F32), pltpu.VMEM((bm, e), F32)],
        compiler_params=_cparams(("arbitrary", "arbitrary")),
        name="conv_gate",
    )(abg, abg, abg, abg, abg, abg, abg, conv_w, conv_b.reshape(1, e),
      ln_g.reshape(1, e), ln_b.reshape(1, e))


def _norm_rope_head(t, gain, cosf, sinf, even):
    ms = jnp.mean(t * t, axis=-1, keepdims=True)
    tn = t * lax.rsqrt(ms + LN_EPS) * gain
    swapped = jnp.where(even, pltpu.roll(tn, HEAD_DIM - 1, 1), pltpu.roll(tn, 1, 1))
    return tn * cosf + swapped * sinf


def _qk_kernel(nq, *refs):
    if nq:
        q_ref, k_ref, v_ref, cos_ref, sin_ref, qg_ref, kg_ref, qo_ref, ko_ref, vo_ref = refs
    else:
        k_ref, v_ref, cos_ref, sin_ref, qg_ref, kg_ref, ko_ref, vo_ref = refs
    cosf = cos_ref[...]
    sinf = sin_ref[...]
    even = (lax.broadcasted_iota(jnp.int32, cosf.shape, 1) % 2) == 0
    for h in range(nq):
        cols = slice(h * HEAD_DIM, (h + 1) * HEAD_DIM)
        qh = _norm_rope_head(q_ref[0, :, cols].astype(F32), qg_ref[...], cosf, sinf, even)
        qo_ref[0, :, cols] = (qh * Q_SCALE).astype(BF16)
    for h in range(N_KV_HEADS):
        cols = slice(h * HEAD_DIM, (h + 1) * HEAD_DIM)
        kh = _norm_rope_head(k_ref[0, :, cols].astype(F32), kg_ref[...], cosf, sinf, even)
        ko_ref[0, :, cols] = kh.astype(BF16)
    vo_ref[0] = v_ref[0]


def _qk_norm_rope(proj, cosf, sinf, q_g, k_g, with_q, bm=256):
    bsz, s, _ = proj.shape
    bm = min(bm, s)
    aw = N_HEADS * HEAD_DIM
    kw = N_KV_HEADS * HEAD_DIM
    koff = (aw // kw) if with_q else 0
    in_specs = []
    out_specs = []
    out_shape = []
    args = []
    if with_q:
        in_specs.append(pl.BlockSpec((1, bm, aw), lambda b, i: (b, i, 0)))
        args.append(proj)
        out_specs.append(pl.BlockSpec((1, bm, aw), lambda b, i: (b, i, 0)))
        out_shape.append(jax.ShapeDtypeStruct((bsz, s, aw), BF16))
    in_specs += [
        pl.BlockSpec((1, bm, kw), lambda b, i: (b, i, koff)),
        pl.BlockSpec((1, bm, kw), lambda b, i: (b, i, koff + 1)),
        pl.BlockSpec((bm, HEAD_DIM), lambda b, i: (i, 0)),
        pl.BlockSpec((bm, HEAD_DIM), lambda b, i: (i, 0)),
        pl.BlockSpec((1, HEAD_DIM), lambda b, i: (0, 0)),
        pl.BlockSpec((1, HEAD_DIM), lambda b, i: (0, 0)),
    ]
    args += [proj, proj, cosf, sinf, q_g.reshape(1, HEAD_DIM), k_g.reshape(1, HEAD_DIM)]
    out_specs += [pl.BlockSpec((1, bm, kw), lambda b, i: (b, i, 0))] * 2
    out_shape += [jax.ShapeDtypeStruct((bsz, s, kw), BF16)] * 2
    return pl.pallas_call(
        functools.partial(_qk_kernel, N_HEADS if with_q else 0),
        grid=(bsz, s // bm),
        in_specs=in_specs,
        out_specs=out_specs,
        out_shape=out_shape,
        compiler_params=_cparams(("arbitrary", "arbitrary")),
        name="qk_norm_rope",
    )(*args)


def _attn_kernel(q_ref, k_ref, v_ref, g_ref, z_ref):
    k = k_ref[0]
    v = v_ref[0]
    for h in range(GQA_GROUP):
        cols = slice(h * HEAD_DIM, (h + 1) * HEAD_DIM)
        s = lax.dot_general(q_ref[0, :, cols], k, (((1,), (1,)), ((), ())), preferred_element_type=F32)
        m = jnp.max(s, axis=-1, keepdims=True)
        p = jnp.exp2(s - m)
        l = jnp.sum(p, axis=-1, keepdims=True)
        o = jnp.dot(p.astype(BF16), v, preferred_element_type=F32) / l
        z_ref[0, :, cols] = (o * _silu(g_ref[0, :, cols].astype(F32))).astype(BF16)


def _attention(q, k_all, v_all, proj, bq=256):
    bsz, s, aw = q.shape
    lk = k_all.shape[1]
    gw = GQA_GROUP * HEAD_DIM
    goff = (proj.shape[2] - aw) // gw
    bq = min(bq, s)
    return pl.pallas_call(
        _attn_kernel,
        grid=(bsz, N_KV_HEADS, s // bq),
        in_specs=[
            pl.BlockSpec((1, bq, gw), lambda b, h, i: (b, i, h)),
            pl.BlockSpec((1, lk, HEAD_DIM), lambda b, h, i: (b, 0, h)),
            pl.BlockSpec((1, lk, HEAD_DIM), lambda b, h, i: (b, 0, h)),
            pl.BlockSpec((1, bq, gw), lambda b, h, i: (b, i, goff + h)),
        ],
        out_specs=pl.BlockSpec((1, bq, gw), lambda b, h, i: (b, i, h)),
        out_shape=jax.ShapeDtypeStruct((bsz, s, aw), BF16),
        compiler_params=_cparams(("arbitrary", "arbitrary", "arbitrary")),
        name="gqa_attention",
    )(q, k_all, v_all, proj)


def _rope_tables(n_tokens):
    rows = n_tokens // GRID_W
    row = jnp.repeat(jnp.arange(rows, dtype=F32), GRID_W)
    col = jnp.tile(jnp.arange(GRID_W, dtype=F32), rows)
    axis_dim = HEAD_DIM // 2
    inv = 1.0 / (ROPE_THETA ** (jnp.arange(0, axis_dim, 2, dtype=F32) / axis_dim))
    ang = jnp.concatenate([row[:, None] * inv, col[:, None] * inv], axis=-1)
    cosf = jnp.repeat(jnp.cos(ang), 2, axis=-1)
    sinf = jnp.repeat(jnp.sin(ang), 2, axis=-1) * jnp.tile(jnp.array([-1.0, 1.0], F32), HEAD_DIM // 2)
    return cosf, sinf


def kernel(x, c, ctx, c_ctx, mod_w, mod_b, post_g, post_b, a_w_in, a_ln_g, a_ln_b, a_w_s, a_b_s, a_w_out, b_w_in, b_conv_w, b_conv_b, b_ln_g, b_ln_b, b_w_out, c_w_in, c_q_g, c_k_g, c_w_out):
    bsz, n_tokens, d = x.shape
    lc = ctx.shape[1]
    assert bsz + 1 <= MOD_ROWS
    kinds = [i % N_MIXERS for i in range(DEPTH)]

    cond = jnp.zeros((MOD_ROWS, d), F32).at[:bsz].set(c).at[bsz].set(c_ctx)
    mods = _modulation(cond, mod_w, mod_b)

    def mod_parts(i, ctx_rows):
        m = mods[i, bsz:bsz + 1] if ctx_rows else mods[i, :bsz]
        m = jnp.broadcast_to(m, (bsz, 3 * d)).reshape(bsz, 1, 3 * d)
        return m[..., :d], m[..., d:2 * d], m[..., 2 * d:]

    aw = N_HEADS * HEAD_DIM
    kw = N_KV_HEADS * HEAD_DIM
    for i in range(DEPTH):
        kind = kinds[i]
        slot = kinds[:i].count(kind)
        ctx_read_later = 2 in kinds[i + 1:]
        streams = [(x, False)] + ([(ctx, True)] if ctx_read_later else [])
        outs = []
        if kind == 0:
            w_in = a_w_in[slot].astype(BF16)
            w_out = a_w_out[slot].astype(BF16)
            w_s = a_w_s[slot].astype(BF16)
            b_s_t = a_b_s[slot].T
            for t, is_ctx in streams:
                shift, scale, gate = mod_parts(i, is_ctx)
                uvg = _mod_matmul(t, scale, shift, w_in)
                z = _sgu_gate(uvg, a_ln_g[slot], a_ln_b[slot], w_s, b_s_t)
                outs.append(_out_matmul_norm(z, w_out, t, gate, post_g[i], post_b[i]))
        elif kind == 1:
            w_in = b_w_in[slot].astype(BF16)
            w_out = b_w_out[slot].astype(BF16)
            for t, is_ctx in streams:
                shift, scale, gate = mod_parts(i, is_ctx)
                abg = _mod_matmul(t, scale, shift, w_in)
                z = _conv_gate(abg, b_conv_w[slot], b_conv_b[slot], b_ln_g[slot], b_ln_b[slot])
                outs.append(_out_matmul_norm(z, w_out, t, gate, post_g[i], post_b[i]))
        else:
            assert not ctx_read_later, "no later layer reads context after the attention layer at this depth"
            w_in = c_w_in[slot].astype(BF16)
            w_out = c_w_out[slot].astype(BF16)
            shift, scale, gate = mod_parts(i, False)
            shift_c, scale_c, _ = mod_parts(i, True)
            proj = _mod_matmul(x, scale, shift, w_in)
            proj_c = _mod_matmul(ctx, scale_c, shift_c, w_in[:, aw:aw + 2 * kw])
            cosf, sinf = _rope_tables(n_tokens)
            q, k, v = _qk_norm_rope(proj, cosf, sinf, c_q_g[slot], c_k_g[slot], True)
            kc, vc = _qk_norm_rope(proj_c, jnp.ones((lc, HEAD_DIM), F32), jnp.zeros((lc, HEAD_DIM), F32),
                                   c_q_g[slot], c_k_g[slot], False)
            k_all = jnp.concatenate([k, kc], axis=1)
            v_all = jnp.concatenate([v, vc], axis=1)
            z = _attention(q, k_all, v_all, proj)
            outs.append(_out_matmul_norm(z, w_out, x, gate, post_g[i], post_b[i]))
        x = outs[0]
        if ctx_read_later:
            ctx = outs[1]
    return x
```

```python
import functools

import jax
import jax.numpy as jnp
from jax import lax
from jax.experimental import pallas as pl
from jax.experimental.pallas import tpu as pltpu

F32 = jnp.float32
BF16 = jnp.bfloat16

DEPTH = 4
N_MIXERS = 3
GRID_W = 64
CHUNK = 128
SGU_GROUPS = 16
CONV_W = 31
CONV_PAD = CONV_W // 2
HALO = 16
HEAD_DIM = 128
N_HEADS = 16
N_KV_HEADS = 4
GQA_GROUP = N_HEADS // N_KV_HEADS
ROPE_THETA = 10000.0
Q_SCALE = HEAD_DIM ** -0.5 * 1.4426950408889634
DEEPNORM_ALPHA = (2 * DEPTH) ** 0.25
LN_EPS = 1e-6
MOD_ROWS = 8
NORM_ROWS = 32

VMEM_LIMIT = 56 * 1024 * 1024


def _cparams(sem):
    return pltpu.CompilerParams(dimension_semantics=sem, vmem_limit_bytes=VMEM_LIMIT)


def _layer_norm(x, g, b):
    mu = jnp.mean(x, axis=-1, keepdims=True)
    xc = x - mu
    var = jnp.mean(xc * xc, axis=-1, keepdims=True)
    return xc * lax.rsqrt(var + LN_EPS) * g + b


def _silu(x):
    return x * jax.nn.sigmoid(x)


def _mod_kernel(c_ref, w_ref, b_ref, o_ref):
    s = _silu(c_ref[...]).astype(BF16)
    o_ref[0] = jnp.dot(s, w_ref[0].astype(BF16), preferred_element_type=F32) + b_ref[0]


def _modulation(cond, mod_w, mod_b):
    depth, d, n = mod_w.shape
    bn = 1024
    return pl.pallas_call(
        _mod_kernel,
        grid=(depth, n // bn),
        in_specs=[
            pl.BlockSpec((MOD_ROWS, d), lambda i, j: (0, 0)),
            pl.BlockSpec((1, d, bn), lambda i, j: (i, 0, j)),
            pl.BlockSpec((1, 1, bn), lambda i, j: (i, 0, j)),
        ],
        out_specs=pl.BlockSpec((1, MOD_ROWS, bn), lambda i, j: (i, 0, j)),
        out_shape=jax.ShapeDtypeStruct((depth, MOD_ROWS, n), F32),
        compiler_params=_cparams(("arbitrary", "arbitrary")),
        name="modulation",
    )(cond, mod_w, mod_b.reshape(depth, 1, n))


def _mm_kernel(x_ref, sc_ref, sh_ref, w_ref, o_ref, h_ref):
    @pl.when(pl.program_id(2) == 0)
    def _():
        h_ref[...] = (x_ref[0] * (1.0 + sc_ref[0]) + sh_ref[0]).astype(BF16)

    o_ref[0] = jnp.dot(h_ref[...], w_ref[0].astype(BF16), preferred_element_type=F32).astype(o_ref.dtype)


def _mod_matmul(x, scale, shift, w, slot, col0=0, n=None, bm=1024, bn=1024):
    bsz, s, d = x.shape
    n = w.shape[2] if n is None else n
    bm = min(bm, s)
    bn = min(bn, n)
    j0 = col0 // bn
    assert col0 % bn == 0 and n % bn == 0 and s % bm == 0
    return pl.pallas_call(
        _mm_kernel,
        grid=(bsz, s // bm, n // bn),
        in_specs=[
            pl.BlockSpec((1, bm, d), lambda b, i, j: (b, i, 0)),
            pl.BlockSpec((1, 1, d), lambda b, i, j: (b, 0, 0)),
            pl.BlockSpec((1, 1, d), lambda b, i, j: (b, 0, 0)),
            pl.BlockSpec((1, d, bn), lambda b, i, j: (slot, 0, j0 + j)),
        ],
        out_specs=pl.BlockSpec((1, bm, bn), lambda b, i, j: (b, i, j)),
        out_shape=jax.ShapeDtypeStruct((bsz, s, n), BF16),
        scratch_shapes=[pltpu.VMEM((bm, d), BF16)],
        compiler_params=_cparams(("arbitrary", "arbitrary", "arbitrary")),
        name="mod_matmul",
    )(x, scale, shift, w)


OUT_SUB_ROWS = 256


def _out_kernel(z_ref, w_ref, x_ref, gate_ref, pg_ref, pb_ref, o_ref):
    bm = o_ref.shape[1]
    sub = min(OUT_SUB_ROWS, bm)
    for r0 in range(0, bm, sub):
        rows = slice(r0, r0 + sub)
        y = jnp.dot(z_ref[0, rows, :], w_ref[...], preferred_element_type=F32)
        o_ref[0, rows, :] = DEEPNORM_ALPHA * x_ref[0, rows, :] + gate_ref[0] * y
        for t0 in range(r0, r0 + sub, NORM_ROWS):
            tile = slice(t0, t0 + NORM_ROWS)
            o_ref[0, tile, :] = _layer_norm(o_ref[0, tile, :], pg_ref[...], pb_ref[...])


def _out_matmul_norm(z, w, x, gate, post_g, post_b, bm=512):
    bsz, s, kdim = z.shape
    d = w.shape[1]
    bm = min(bm, s)
    return pl.pallas_call(
        _out_kernel,
        grid=(bsz, s // bm),
        in_specs=[
            pl.BlockSpec((1, bm, kdim), lambda b, i: (b, i, 0)),
            pl.BlockSpec((kdim, d), lambda b, i: (0, 0), pipeline_mode=pl.Buffered(1)),
            pl.BlockSpec((1, bm, d), lambda b, i: (b, i, 0)),
            pl.BlockSpec((1, 1, d), lambda b, i: (b, 0, 0)),
            pl.BlockSpec((1, d), lambda b, i: (0, 0)),
            pl.BlockSpec((1, d), lambda b, i: (0, 0)),
        ],
        out_specs=pl.BlockSpec((1, bm, d), lambda b, i: (b, i, 0)),
        out_shape=jax.ShapeDtypeStruct((bsz, s, d), F32),
        compiler_params=_cparams(("arbitrary", "arbitrary")),
        name="out_matmul_norm",
    )(z, w, x, gate, post_g.reshape(1, d), post_b.reshape(1, d))


def _sgu_kernel(u_ref, v_ref, g_ref, lg_ref, lb_ref, ws_ref, bs_ref, z_ref, vn_ref):
    bm = v_ref.shape[1]
    gw = v_ref.shape[2] // SGU_GROUPS
    for c in range(bm // CHUNK):
        rows = slice(c * CHUNK, (c + 1) * CHUNK)
        vn_ref[...] = _layer_norm(v_ref[0, rows, :].astype(F32), lg_ref[...], lb_ref[...]).astype(BF16)
        for h in range(SGU_GROUPS):
            cols = slice(h * gw, (h + 1) * gw)
            s = jnp.dot(ws_ref[h], vn_ref[:, cols], preferred_element_type=F32) + bs_ref[:, h:h + 1]
            u = u_ref[0, rows, cols].astype(F32)
            z_ref[0, rows, cols] = (u * s * _silu(g_ref[0, rows, cols].astype(F32))).astype(BF16)


def _sgu_gate(uvg, ln_g, ln_b, w_s, b_s_t, bm=512):
    bsz, s, n3 = uvg.shape
    e = n3 // 3
    bm = min(bm, s)
    return pl.pallas_call(
        _sgu_kernel,
        grid=(bsz, s // bm),
        in_specs=[
            pl.BlockSpec((1, bm, e), lambda b, i: (b, i, 0)),
            pl.BlockSpec((1, bm, e), lambda b, i: (b, i, 1)),
            pl.BlockSpec((1, bm, e), lambda b, i: (b, i, 2)),
            pl.BlockSpec((1, e), lambda b, i: (0, 0)),
            pl.BlockSpec((1, e), lambda b, i: (0, 0)),
            pl.BlockSpec((SGU_GROUPS, CHUNK, CHUNK), lambda b, i: (0, 0, 0)),
            pl.BlockSpec((CHUNK, SGU_GROUPS), lambda b, i: (0, 0)),
        ],
        out_specs=pl.BlockSpec((1, bm, e), lambda b, i: (b, i, 0)),
        out_shape=jax.ShapeDtypeStruct((bsz, s, e), BF16),
        scratch_shapes=[pltpu.VMEM((CHUNK, e), BF16)],
        compiler_params=_cparams(("arbitrary", "arbitrary")),
        name="sgu_gate",
    )(uvg, uvg, uvg, ln_g.reshape(1, e), ln_b.reshape(1, e), w_s, b_s_t)


CONV_ROWS = 128
CONV_COLS = 128
SUBLANES = 8


def _glu(a, b):
    return a.astype(F32) * jax.nn.sigmoid(b.astype(F32))


def _conv_kernel(a_ref, b_ref, g_ref, ap_ref, bp_ref, an_ref, bn_ref,
                 cw_ref, cb_ref, lg_ref, lb_ref, z_ref, yext_ref, yc_ref):
    i = pl.program_id(1)
    bm = a_ref.shape[1]
    e = a_ref.shape[2]
    yext_ref[HALO:HALO + bm, :] = _glu(a_ref[0], b_ref[0])
    yext_ref[0:HALO, :] = jnp.where(i > 0, _glu(ap_ref[0], bp_ref[0]), 0.0)
    yext_ref[HALO + bm:HALO + bm + HALO, :] = jnp.where(
        i < pl.num_programs(1) - 1, _glu(an_ref[0], bn_ref[0]), 0.0)

    assert HALO - CONV_PAD == 1
    rc = min(CONV_ROWS, bm)
    for r0 in range(0, bm, rc):
        for c0 in range(0, e, CONV_COLS):
            cols = slice(c0, c0 + CONV_COLS)
            acc = jnp.broadcast_to(cb_ref[:, cols], (rc, CONV_COLS))
            for r in range(SUBLANES):
                part = None
                for j in range(r if r else SUBLANES, CONV_W + 1, SUBLANES):
                    lo = r0 + j - r
                    term = cw_ref[j - 1:j, cols] * yext_ref[lo:lo + rc + SUBLANES, cols]
                    part = term if part is None else part + term
                acc = acc + part[r:r + rc]
            yc_ref[r0:r0 + rc, cols] = acc

    def norm_tile(t, carry):
        rows = pl.ds(pl.multiple_of(t * NORM_ROWS, NORM_ROWS), NORM_ROWS)
        y = _silu(_layer_norm(yc_ref[rows, :], lg_ref[...], lb_ref[...]))
        z_ref[0, rows, :] = (y * _silu(g_ref[0, rows, :].astype(F32))).astype(BF16)
        return carry

    lax.fori_loop(0, bm // NORM_ROWS, norm_tile, 0)


def _conv_gate(abg, conv_w, conv_b, ln_g, ln_b, bm=256):
    bsz, s, n3 = abg.shape
    e = n3 // 3
    bm = min(bm, s)
    hb = bm // HALO
    last = s // HALO - 1

    def prev_map(col):
        return lambda b, i: (b, jnp.maximum(i * hb - 1, 0), col)

    def next_map(col):
        return lambda b, i: (b, jnp.minimum((i + 1) * hb, last), col)

    return pl.pallas_call(
        _conv_kernel,
        grid=(bsz, s // bm),
        in_specs=[
            pl.BlockSpec((1, bm, e), lambda b, i: (b, i, 0)),
            pl.BlockSpec((1, bm, e), lambda b, i: (b, i, 1)),
            pl.BlockSpec((1, bm, e), lambda b, i: (b, i, 2)),
            pl.BlockSpec((1, HALO, e), prev_map(0)),
            pl.BlockSpec((1, HALO, e), prev_map(1)),
            pl.BlockSpec((1, HALO, e), next_map(0)),
            pl.BlockSpec((1, HALO, e), next_map(1)),
            pl.BlockSpec((CONV_W, e), lambda b, i: (0, 0)),
            pl.BlockSpec((1, e), lambda b, i: (0, 0)),
            pl.BlockSpec((1, e), lambda b, i: (0, 0)),
            pl.BlockSpec((1, e), lambda b, i: (0, 0)),
        ],
        out_specs=pl.BlockSpec((1, bm, e), lambda b, i: (b, i, 0)),
        out_shape=jax.ShapeDtypeStruct((bsz, s, e), BF16),
        scratch_shapes=[pltpu.VMEM((bm + 2 * HALO, e), F32), pltpu.VMEM((bm, e), F32)],
        compiler_params=_cparams(("arbitrary", "arbitrary")),
        name="conv_gate",
    )(abg, abg, abg, abg, abg, abg, abg, conv_w, conv_b.reshape(1, e),
      ln_g.reshape(1, e), ln_b.reshape(1, e))


def _norm_rope_head(t, gain, cosf, sinf, even):
    ms = jnp.mean(t * t, axis=-1, keepdims=True)
    tn = t * lax.rsqrt(ms + LN_EPS) * gain
    swapped = jnp.where(even, pltpu.roll(tn, HEAD_DIM - 1, 1), pltpu.roll(tn, 1, 1))
    return tn * cosf + swapped * sinf


def _qk_kernel(nq, *refs):
    if nq:
        q_ref, k_ref, v_ref, cos_ref, sin_ref, qg_ref, kg_ref, qo_ref, ko_ref, vo_ref = refs
    else:
        k_ref, v_ref, cos_ref, sin_ref, qg_ref, kg_ref, ko_ref, vo_ref = refs
    cosf = cos_ref[...]
    sinf = sin_ref[...]
    even = (lax.broadcasted_iota(jnp.int32, cosf.shape, 1) % 2) == 0
    for h in range(nq):
        cols = slice(h * HEAD_DIM, (h + 1) * HEAD_DIM)
        qh = _norm_rope_head(q_ref[0, :, cols].astype(F32), qg_ref[...], cosf, sinf, even)
        qo_ref[0, :, cols] = (qh * Q_SCALE).astype(BF16)
    for h in range(N_KV_HEADS):
        cols = slice(h * HEAD_DIM, (h + 1) * HEAD_DIM)
        kh = _norm_rope_head(k_ref[0, :, cols].astype(F32), kg_ref[...], cosf, sinf, even)
        ko_ref[0, :, cols] = kh.astype(BF16)
    vo_ref[0] = v_ref[0]


def _qk_norm_rope(proj, cosf, sinf, q_g, k_g, with_q, bm=256):
    bsz, s, _ = proj.shape
    bm = min(bm, s)
    aw = N_HEADS * HEAD_DIM
    kw = N_KV_HEADS * HEAD_DIM
    koff = (aw // kw) if with_q else 0
    in_specs = []
    out_specs = []
    out_shape = []
    args = []
    if with_q:
        in_specs.append(pl.BlockSpec((1, bm, aw), lambda b, i: (b, i, 0)))
        args.append(proj)
        out_specs.append(pl.BlockSpec((1, bm, aw), lambda b, i: (b, i, 0)))
        out_shape.append(jax.ShapeDtypeStruct((bsz, s, aw), BF16))
    in_specs += [
        pl.BlockSpec((1, bm, kw), lambda b, i: (b, i, koff)),
        pl.BlockSpec((1, bm, kw), lambda b, i: (b, i, koff + 1)),
        pl.BlockSpec((bm, HEAD_DIM), lambda b, i: (i, 0)),
        pl.BlockSpec((bm, HEAD_DIM), lambda b, i: (i, 0)),
        pl.BlockSpec((1, HEAD_DIM), lambda b, i: (0, 0)),
        pl.BlockSpec((1, HEAD_DIM), lambda b, i: (0, 0)),
    ]
    args += [proj, proj, cosf, sinf, q_g.reshape(1, HEAD_DIM), k_g.reshape(1, HEAD_DIM)]
    out_specs += [pl.BlockSpec((1, bm, kw), lambda b, i: (b, i, 0))] * 2
    out_shape += [jax.ShapeDtypeStruct((bsz, s, kw), BF16)] * 2
    return pl.pallas_call(
        functools.partial(_qk_kernel, N_HEADS if with_q else 0),
        grid=(bsz, s // bm),
        in_specs=in_specs,
        out_specs=out_specs,
        out_shape=out_shape,
        compiler_params=_cparams(("arbitrary", "arbitrary")),
        name="qk_norm_rope",
    )(*args)


def _attn_kernel(q_ref, k_ref, v_ref, g_ref, z_ref):
    k = k_ref[0]
    v = v_ref[0]
    for h in range(GQA_GROUP):
        cols = slice(h * HEAD_DIM, (h + 1) * HEAD_DIM)
        s = lax.dot_general(q_ref[0, :, cols], k, (((1,), (1,)), ((), ())), preferred_element_type=F32)
        m = jnp.max(s, axis=-1, keepdims=True)
        p = jnp.exp2(s - m)
        l = jnp.sum(p, axis=-1, keepdims=True)
        o = jnp.dot(p.astype(BF16), v, preferred_element_type=F32) / l
        z_ref[0, :, cols] = (o * _silu(g_ref[0, :, cols].astype(F32))).astype(BF16)


def _attention(q, k_all, v_all, proj, bq=256):
    bsz, s, aw = q.shape
    lk = k_all.shape[1]
    gw = GQA_GROUP * HEAD_DIM
    goff = (proj.shape[2] - aw) // gw
    bq = min(bq, s)
    return pl.pallas_call(
        _attn_kernel,
        grid=(bsz, N_KV_HEADS, s // bq),
        in_specs=[
            pl.BlockSpec((1, bq, gw), lambda b, h, i: (b, i, h)),
            pl.BlockSpec((1, lk, HEAD_DIM), lambda b, h, i: (b, 0, h)),
            pl.BlockSpec((1, lk, HEAD_DIM), lambda b, h, i: (b, 0, h)),
            pl.BlockSpec((1, bq, gw), lambda b, h, i: (b, i, goff + h)),
        ],
        out_specs=pl.BlockSpec((1, bq, gw), lambda b, h, i: (b, i, h)),
        out_shape=jax.ShapeDtypeStruct((bsz, s, aw), BF16),
        compiler_params=_cparams(("arbitrary", "arbitrary", "arbitrary")),
        name="gqa_attention",
    )(q, k_all, v_all, proj)


def _rope_tables(n_tokens):
    rows = n_tokens // GRID_W
    row = jnp.repeat(jnp.arange(rows, dtype=F32), GRID_W)
    col = jnp.tile(jnp.arange(GRID_W, dtype=F32), rows)
    axis_dim = HEAD_DIM // 2
    inv = 1.0 / (ROPE_THETA ** (jnp.arange(0, axis_dim, 2, dtype=F32) / axis_dim))
    ang = jnp.concatenate([row[:, None] * inv, col[:, None] * inv], axis=-1)
    cosf = jnp.repeat(jnp.cos(ang), 2, axis=-1)
    sinf = jnp.repeat(jnp.sin(ang), 2, axis=-1) * jnp.tile(jnp.array([-1.0, 1.0], F32), HEAD_DIM // 2)
    return cosf, sinf


def kernel(x, c, ctx, c_ctx, mod_w, mod_b, post_g, post_b, a_w_in, a_ln_g, a_ln_b, a_w_s, a_b_s, a_w_out, b_w_in, b_conv_w, b_conv_b, b_ln_g, b_ln_b, b_w_out, c_w_in, c_q_g, c_k_g, c_w_out):
    bsz, n_tokens, d = x.shape
    lc = ctx.shape[1]
    assert bsz + 1 <= MOD_ROWS
    kinds = [i % N_MIXERS for i in range(DEPTH)]

    cond = jnp.zeros((MOD_ROWS, d), F32).at[:bsz].set(c).at[bsz].set(c_ctx)
    mods = _modulation(cond, mod_w, mod_b)

    def mod_parts(i, ctx_rows):
        m = mods[i, bsz:bsz + 1] if ctx_rows else mods[i, :bsz]
        m = jnp.broadcast_to(m, (bsz, 3 * d)).reshape(bsz, 1, 3 * d)
        return m[..., :d], m[..., d:2 * d], m[..., 2 * d:]

    def in_proj(t, i, is_ctx, w, slot, **cols):
        shift, scale, _ = mod_parts(i, is_ctx)
        if is_ctx:
            flat = t.reshape(1, bsz * lc, d)
            return _mod_matmul(flat, scale[:1], shift[:1], w, slot, **cols).reshape(bsz, lc, -1)
        return _mod_matmul(t, scale, shift, w, slot, **cols)

    aw = N_HEADS * HEAD_DIM
    kw = N_KV_HEADS * HEAD_DIM
    for i in range(DEPTH):
        kind = kinds[i]
        slot = kinds[:i].count(kind)
        ctx_read_later = 2 in kinds[i + 1:]
        streams = [(x, False)] + ([(ctx, True)] if ctx_read_later else [])
        outs = []
        if kind == 0:
            w_out = a_w_out[slot].astype(BF16)
            w_s = a_w_s[slot].astype(BF16)
            b_s_t = a_b_s[slot].T
            for t, is_ctx in streams:
                gate = mod_parts(i, is_ctx)[2]
                uvg = in_proj(t, i, is_ctx, a_w_in, slot)
                z = _sgu_gate(uvg, a_ln_g[slot], a_ln_b[slot], w_s, b_s_t)
                outs.append(_out_matmul_norm(z, w_out, t, gate, post_g[i], post_b[i]))
        elif kind == 1:
            w_out = b_w_out[slot].astype(BF16)
            for t, is_ctx in streams:
                gate = mod_parts(i, is_ctx)[2]
                abg = in_proj(t, i, is_ctx, b_w_in, slot)
                z = _conv_gate(abg, b_conv_w[slot], b_conv_b[slot], b_ln_g[slot], b_ln_b[slot])
                outs.append(_out_matmul_norm(z, w_out, t, gate, post_g[i], post_b[i]))
        else:
            assert not ctx_read_later, "no later layer reads context after the attention layer at this depth"
            w_out = c_w_out[slot].astype(BF16)
            gate = mod_parts(i, False)[2]
            proj = in_proj(x, i, False, c_w_in, slot)
            proj_c = in_proj(ctx, i, True, c_w_in, slot, col0=aw, n=2 * kw)
            cosf, sinf = _rope_tables(n_tokens)
            q, k, v = _qk_norm_rope(proj, cosf, sinf, c_q_g[slot], c_k_g[slot], True)
            kc, vc = _qk_norm_rope(proj_c, jnp.ones((lc, HEAD_DIM), F32), jnp.zeros((lc, HEAD_DIM), F32),
                                   c_q_g[slot], c_k_g[slot], False)
            k_all = jnp.concatenate([k, kc], axis=1)
            v_all = jnp.concatenate([v, vc], axis=1)
            z = _attention(q, k_all, v_all, proj)
            outs.append(_out_matmul_norm(z, w_out, x, gate, post_g[i], post_b[i]))
        x = outs[0]
        if ctx_read_later:
            ctx = outs[1]
    return x
```

```python
import functools

import jax
import jax.numpy as jnp
from jax import lax
from jax.experimental import pallas as pl
from jax.experimental.pallas import tpu as pltpu

F32 = jnp.float32
BF16 = jnp.bfloat16

DEPTH = 4
N_MIXERS = 3
GRID_W = 64
CHUNK = 128
SGU_GROUPS = 16
CONV_W = 31
CONV_PAD = CONV_W // 2
HALO = 16
HEAD_DIM = 128
N_HEADS = 16
N_KV_HEADS = 4
GQA_GROUP = N_HEADS // N_KV_HEADS
ROPE_THETA = 10000.0
Q_SCALE = HEAD_DIM ** -0.5 * 1.4426950408889634
DEEPNORM_ALPHA = (2 * DEPTH) ** 0.25
LN_EPS = 1e-6
MOD_ROWS = 8
NORM_ROWS = 32

VMEM_LIMIT = 56 * 1024 * 1024


def _cparams(sem):
    return pltpu.CompilerParams(dimension_semantics=sem, vmem_limit_bytes=VMEM_LIMIT)


def _layer_norm(x, g, b):
    mu = jnp.mean(x, axis=-1, keepdims=True)
    xc = x - mu
    var = jnp.mean(xc * xc, axis=-1, keepdims=True)
    return xc * lax.rsqrt(var + LN_EPS) * g + b


def _silu(x):
    return x * jax.nn.sigmoid(x)


def _mod_kernel(c_ref, w_ref, b_ref, o_ref):
    s = _silu(c_ref[...]).astype(BF16)
    o_ref[0] = jnp.dot(s, w_ref[0].astype(BF16), preferred_element_type=F32) + b_ref[0]


def _modulation(cond, mod_w, mod_b):
    depth, d, n = mod_w.shape
    bn = 1024
    return pl.pallas_call(
        _mod_kernel,
        grid=(depth, n // bn),
        in_specs=[
            pl.BlockSpec((MOD_ROWS, d), lambda i, j: (0, 0)),
            pl.BlockSpec((1, d, bn), lambda i, j: (i, 0, j)),
            pl.BlockSpec((1, 1, bn), lambda i, j: (i, 0, j)),
        ],
        out_specs=pl.BlockSpec((1, MOD_ROWS, bn), lambda i, j: (i, 0, j)),
        out_shape=jax.ShapeDtypeStruct((depth, MOD_ROWS, n), F32),
        compiler_params=_cparams(("arbitrary", "arbitrary")),
        name="modulation",
    )(cond, mod_w, mod_b.reshape(depth, 1, n))


def _mm_kernel(x_ref, sc_ref, sh_ref, w_ref, o_ref, h_ref):
    @pl.when(pl.program_id(2) == 0)
    def _():
        h_ref[...] = (x_ref[0] * (1.0 + sc_ref[0]) + sh_ref[0]).astype(BF16)

    o_ref[0] = jnp.dot(h_ref[...], w_ref[0].astype(BF16), preferred_element_type=F32).astype(o_ref.dtype)


def _mod_matmul(x, scale, shift, w, slot, col0=0, n=None, bm=1024, bn=1024):
    bsz, s, d = x.shape
    n = w.shape[2] if n is None else n
    bm = min(bm, s)
    bn = min(bn, n)
    j0 = col0 // bn
    assert col0 % bn == 0 and n % bn == 0 and s % bm == 0
    return pl.pallas_call(
        _mm_kernel,
        grid=(bsz, s // bm, n // bn),
        in_specs=[
            pl.BlockSpec((1, bm, d), lambda b, i, j: (b, i, 0)),
            pl.BlockSpec((1, 1, d), lambda b, i, j: (b, 0, 0)),
            pl.BlockSpec((1, 1, d), lambda b, i, j: (b, 0, 0)),
            pl.BlockSpec((1, d, bn), lambda b, i, j: (slot, 0, j0 + j)),
        ],
        out_specs=pl.BlockSpec((1, bm, bn), lambda b, i, j: (b, i, j)),
        out_shape=jax.ShapeDtypeStruct((bsz, s, n), BF16),
        scratch_shapes=[pltpu.VMEM((bm, d), BF16)],
        compiler_params=_cparams(("arbitrary", "arbitrary", "arbitrary")),
        name="mod_matmul",
    )(x, scale, shift, w)


OUT_SUB_ROWS = 256


def _out_kernel(z_ref, w_ref, x_ref, gate_ref, pg_ref, pb_ref, o_ref):
    bm = o_ref.shape[1]
    sub = min(OUT_SUB_ROWS, bm)
    for r0 in range(0, bm, sub):
        rows = slice(r0, r0 + sub)
        y = jnp.dot(z_ref[0, rows, :], w_ref[...], preferred_element_type=F32)
        o_ref[0, rows, :] = DEEPNORM_ALPHA * x_ref[0, rows, :] + gate_ref[0] * y
        for t0 in range(r0, r0 + sub, NORM_ROWS):
            tile = slice(t0, t0 + NORM_ROWS)
            o_ref[0, tile, :] = _layer_norm(o_ref[0, tile, :], pg_ref[...], pb_ref[...])


def _out_matmul_norm(z, w, x, gate, post_g, post_b, bm=512):
    bsz, s, kdim = z.shape
    d = w.shape[1]
    bm = min(bm, s)
    return pl.pallas_call(
        _out_kernel,
        grid=(bsz, s // bm),
        in_specs=[
            pl.BlockSpec((1, bm, kdim), lambda b, i: (b, i, 0)),
            pl.BlockSpec((kdim, d), lambda b, i: (0, 0), pipeline_mode=pl.Buffered(1)),
            pl.BlockSpec((1, bm, d), lambda b, i: (b, i, 0)),
            pl.BlockSpec((1, 1, d), lambda b, i: (b, 0, 0)),
            pl.BlockSpec((1, d), lambda b, i: (0, 0)),
            pl.BlockSpec((1, d), lambda b, i: (0, 0)),
        ],
        out_specs=pl.BlockSpec((1, bm, d), lambda b, i: (b, i, 0)),
        out_shape=jax.ShapeDtypeStruct((bsz, s, d), F32),
        compiler_params=_cparams(("arbitrary", "arbitrary")),
        name="out_matmul_norm",
    )(z, w, x, gate, post_g.reshape(1, d), post_b.reshape(1, d))


def _sgu_kernel(u_ref, v_ref, g_ref, lg_ref, lb_ref, ws_ref, bs_ref, z_ref, vn_ref):
    bm = v_ref.shape[1]
    gw = v_ref.shape[2] // SGU_GROUPS
    for c in range(bm // CHUNK):
        rows = slice(c * CHUNK, (c + 1) * CHUNK)
        vn_ref[...] = _layer_norm(v_ref[0, rows, :].astype(F32), lg_ref[...], lb_ref[...]).astype(BF16)
        for h in range(SGU_GROUPS):
            cols = slice(h * gw, (h + 1) * gw)
            s = jnp.dot(ws_ref[h], vn_ref[:, cols], preferred_element_type=F32) + bs_ref[:, h:h + 1]
            u = u_ref[0, rows, cols].astype(F32)
            z_ref[0, rows, cols] = (u * s * _silu(g_ref[0, rows, cols].astype(F32))).astype(BF16)


def _sgu_gate(uvg, ln_g, ln_b, w_s, b_s_t, bm=512):
    bsz, s, n3 = uvg.shape
    e = n3 // 3
    bm = min(bm, s)
    return pl.pallas_call(
        _sgu_kernel,
        grid=(bsz, s // bm),
        in_specs=[
            pl.BlockSpec((1, bm, e), lambda b, i: (b, i, 0)),
            pl.BlockSpec((1, bm, e), lambda b, i: (b, i, 1)),
            pl.BlockSpec((1, bm, e), lambda b, i: (b, i, 2)),
            pl.BlockSpec((1, e), lambda b, i: (0, 0)),
            pl.BlockSpec((1, e), lambda b, i: (0, 0)),
            pl.BlockSpec((SGU_GROUPS, CHUNK, CHUNK), lambda b, i: (0, 0, 0)),
            pl.BlockSpec((CHUNK, SGU_GROUPS), lambda b, i: (0, 0)),
        ],
        out_specs=pl.BlockSpec((1, bm, e), lambda b, i: (b, i, 0)),
        out_shape=jax.ShapeDtypeStruct((bsz, s, e), BF16),
        scratch_shapes=[pltpu.VMEM((CHUNK, e), BF16)],
        compiler_params=_cparams(("arbitrary", "arbitrary")),
        name="sgu_gate",
    )(uvg, uvg, uvg, ln_g.reshape(1, e), ln_b.reshape(1, e), w_s, b_s_t)


CONV_ROWS = 128
CONV_COLS = 128
SUBLANES = 8


def _glu(a, b):
    return a.astype(F32) * jax.nn.sigmoid(b.astype(F32))


def _conv_kernel(a_ref, b_ref, g_ref, ap_ref, bp_ref, an_ref, bn_ref,
                 cw_ref, cb_ref, lg_ref, lb_ref, z_ref, yext_ref, yc_ref):
    i = pl.program_id(1)
    bm = a_ref.shape[1]
    e = a_ref.shape[2]
    yext_ref[HALO:HALO + bm, :] = _glu(a_ref[0], b_ref[0])
    yext_ref[0:HALO, :] = jnp.where(i > 0, _glu(ap_ref[0], bp_ref[0]), 0.0)
    yext_ref[HALO + bm:HALO + bm + HALO, :] = jnp.where(
        i < pl.num_programs(1) - 1, _glu(an_ref[0], bn_ref[0]), 0.0)

    assert HALO - CONV_PAD == 1
    rc = min(CONV_ROWS, bm)
    for r0 in range(0, bm, rc):
        for c0 in range(0, e, CONV_COLS):
            cols = slice(c0, c0 + CONV_COLS)
            acc = jnp.broadcast_to(cb_ref[:, cols], (rc, CONV_COLS))
            for r in range(SUBLANES):
                part = None
                for j in range(r if r else SUBLANES, CONV_W + 1, SUBLANES):
                    lo = r0 + j - r
                    term = cw_ref[j - 1:j, cols] * yext_ref[lo:lo + rc + SUBLANES, cols]
                    part = term if part is None else part + term
                acc = acc + part[r:r + rc]
            yc_ref[r0:r0 + rc, cols] = acc

    def norm_tile(t, carry):
        rows = pl.ds(pl.multiple_of(t * NORM_ROWS, NORM_ROWS), NORM_ROWS)
        y = _silu(_layer_norm(yc_ref[rows, :], lg_ref[...], lb_ref[...]))
        z_ref[0, rows, :] = (y * _silu(g_ref[0, rows, :].astype(F32))).astype(BF16)
        return carry

    lax.fori_loop(0, bm // NORM_ROWS, norm_tile, 0)


def _conv_gate(abg, conv_w, conv_b, ln_g, ln_b, bm=256):
    bsz, s, n3 = abg.shape
    e = n3 // 3
    bm = min(bm, s)
    hb = bm // HALO
    last = s // HALO - 1

    def prev_map(col):
        return lambda b, i: (b, jnp.maximum(i * hb - 1, 0), col)

    def next_map(col):
        return lambda b, i: (b, jnp.minimum((i + 1) * hb, last), col)

    return pl.pallas_call(
        _conv_kernel,
        grid=(bsz, s // bm),
        in_specs=[
            pl.BlockSpec((1, bm, e), lambda b, i: (b, i, 0)),
            pl.BlockSpec((1, bm, e), lambda b, i: (b, i, 1)),
            pl.BlockSpec((1, bm, e), lambda b, i: (b, i, 2)),
            pl.BlockSpec((1, HALO, e), prev_map(0)),
            pl.BlockSpec((1, HALO, e), prev_map(1)),
            pl.BlockSpec((1, HALO, e), next_map(0)),
            pl.BlockSpec((1, HALO, e), next_map(1)),
            pl.BlockSpec((CONV_W, e), lambda b, i: (0, 0)),
            pl.BlockSpec((1, e), lambda b, i: (0, 0)),
            pl.BlockSpec((1, e), lambda b, i: (0, 0)),
            pl.BlockSpec((1, e), lambda b, i: (0, 0)),
        ],
        out_specs=pl.BlockSpec((1, bm, e), lambda b, i: (b, i, 0)),
        out_shape=jax.ShapeDtypeStruct((bsz, s, e), BF16),
        scratch_shapes=[pltpu.VMEM((bm + 2 * HALO, e), F32), pltpu.VMEM((bm, e), F32)],
        compiler_params=_cparams(("arbitrary", "arbitrary")),
        name="conv_gate",
    )(abg, abg, abg, abg, abg, abg, abg, conv_w, conv_b.reshape(1, e),
      ln_g.reshape(1, e), ln_b.reshape(1, e))


def _norm_rope_head(t, gain, cosf, sinf, even):
    ms = jnp.mean(t * t, axis=-1, keepdims=True)
    tn = t * lax.rsqrt(ms + LN_EPS) * gain
    swapped = jnp.where(even, pltpu.roll(tn, HEAD_DIM - 1, 1), pltpu.roll(tn, 1, 1))
    return tn * cosf + swapped * sinf


def _qk_kernel(nq, *refs):
    if nq:
        q_ref, k_ref, v_ref, cos_ref, sin_ref, qg_ref, kg_ref, qo_ref, ko_ref, vo_ref = refs
    else:
        k_ref, v_ref, cos_ref, sin_ref, qg_ref, kg_ref, ko_ref, vo_ref = refs
    cosf = cos_ref[...]
    sinf = sin_ref[...]
    even = (lax.broadcasted_iota(jnp.int32, cosf.shape, 1) % 2) == 0
    for h in range(nq):
        cols = slice(h * HEAD_DIM, (h + 1) * HEAD_DIM)
        qh = _norm_rope_head(q_ref[0, :, cols].astype(F32), qg_ref[...], cosf, sinf, even)
        qo_ref[0, :, cols] = (qh * Q_SCALE).astype(BF16)
    for h in range(N_KV_HEADS):
        cols = slice(h * HEAD_DIM, (h + 1) * HEAD_DIM)
        kh = _norm_rope_head(k_ref[0, :, cols].astype(F32), kg_ref[...], cosf, sinf, even)
        ko_ref[0, :, cols] = kh.astype(BF16)
    vo_ref[0] = v_ref[0]


def _qk_norm_rope(proj, cosf, sinf, q_g, k_g, with_q, bm=256):
    bsz, s, _ = proj.shape
    bm = min(bm, s)
    aw = N_HEADS * HEAD_DIM
    kw = N_KV_HEADS * HEAD_DIM
    koff = (aw // kw) if with_q else 0
    in_specs = []
    out_specs = []
    out_shape = []
    args = []
    if with_q:
        in_specs.append(pl.BlockSpec((1, bm, aw), lambda b, i: (b, i, 0)))
        args.append(proj)
        out_specs.append(pl.BlockSpec((1, bm, aw), lambda b, i: (b, i, 0)))
        out_shape.append(jax.ShapeDtypeStruct((bsz, s, aw), BF16))
    in_specs += [
        pl.BlockSpec((1, bm, kw), lambda b, i: (b, i, koff)),
        pl.BlockSpec((1, bm, kw), lambda b, i: (b, i, koff + 1)),
        pl.BlockSpec((bm, HEAD_DIM), lambda b, i: (i, 0)),
        pl.BlockSpec((bm, HEAD_DIM), lambda b, i: (i, 0)),
        pl.BlockSpec((1, HEAD_DIM), lambda b, i: (0, 0)),
        pl.BlockSpec((1, HEAD_DIM), lambda b, i: (0, 0)),
    ]
    args += [proj, proj, cosf, sinf, q_g.reshape(1, HEAD_DIM), k_g.reshape(1, HEAD_DIM)]
    out_specs += [pl.BlockSpec((1, bm, kw), lambda b, i: (b, i, 0))] * 2
    out_shape += [jax.ShapeDtypeStruct((bsz, s, kw), BF16)] * 2
    return pl.pallas_call(
        functools.partial(_qk_kernel, N_HEADS if with_q else 0),
        grid=(bsz, s // bm),
        in_specs=in_specs,
        out_specs=out_specs,
        out_shape=out_shape,
        compiler_params=_cparams(("arbitrary", "arbitrary")),
        name="qk_norm_rope",
    )(*args)


def _attn_kernel(q_ref, k_ref, v_ref, g_ref, z_ref):
    k = k_ref[0]
    vx = jnp.concatenate([v_ref[0], jnp.ones_like(v_ref[0])], axis=1)
    for h in range(GQA_GROUP):
        cols = slice(h * HEAD_DIM, (h + 1) * HEAD_DIM)
        s = lax.dot_general(q_ref[0, :, cols], k, (((1,), (1,)), ((), ())), preferred_element_type=F32)
        m = jnp.max(s, axis=-1, keepdims=True)
        p = jnp.exp2((s - m).astype(BF16))
        ox = jnp.dot(p, vx, preferred_element_type=F32)
        o = ox[:, :HEAD_DIM] / ox[:, HEAD_DIM:]
        z_ref[0, :, cols] = (o * _silu(g_ref[0, :, cols].astype(F32))).astype(BF16)


def _attention(q, k_all, v_all, proj, bq=256):
    bsz, s, aw = q.shape
    lk = k_all.shape[1]
    gw = GQA_GROUP * HEAD_DIM
    goff = (proj.shape[2] - aw) // gw
    bq = min(bq, s)
    return pl.pallas_call(
        _attn_kernel,
        grid=(bsz, N_KV_HEADS, s // bq),
        in_specs=[
            pl.BlockSpec((1, bq, gw), lambda b, h, i: (b, i, h)),
            pl.BlockSpec((1, lk, HEAD_DIM), lambda b, h, i: (b, 0, h)),
            pl.BlockSpec((1, lk, HEAD_DIM), lambda b, h, i: (b, 0, h)),
            pl.BlockSpec((1, bq, gw), lambda b, h, i: (b, i, goff + h)),
        ],
        out_specs=pl.BlockSpec((1, bq, gw), lambda b, h, i: (b, i, h)),
        out_shape=jax.ShapeDtypeStruct((bsz, s, aw), BF16),
        compiler_params=_cparams(("arbitrary", "arbitrary", "arbitrary")),
        name="gqa_attention",
    )(q, k_all, v_all, proj)


def _rope_tables(n_tokens):
    rows = n_tokens // GRID_W
    row = jnp.repeat(jnp.arange(rows, dtype=F32), GRID_W)
    col = jnp.tile(jnp.arange(GRID_W, dtype=F32), rows)
    axis_dim = HEAD_DIM // 2
    inv = 1.0 / (ROPE_THETA ** (jnp.arange(0, axis_dim, 2, dtype=F32) / axis_dim))
    ang = jnp.concatenate([row[:, None] * inv, col[:, None] * inv], axis=-1)
    cosf = jnp.repeat(jnp.cos(ang), 2, axis=-1)
    sinf = jnp.repeat(jnp.sin(ang), 2, axis=-1) * jnp.tile(jnp.array([-1.0, 1.0], F32), HEAD_DIM // 2)
    return cosf, sinf


def kernel(x, c, ctx, c_ctx, mod_w, mod_b, post_g, post_b, a_w_in, a_ln_g, a_ln_b, a_w_s, a_b_s, a_w_out, b_w_in, b_conv_w, b_conv_b, b_ln_g, b_ln_b, b_w_out, c_w_in, c_q_g, c_k_g, c_w_out):
    bsz, n_tokens, d = x.shape
    lc = ctx.shape[1]
    assert bsz + 1 <= MOD_ROWS
    kinds = [i % N_MIXERS for i in range(DEPTH)]

    cond = jnp.zeros((MOD_ROWS, d), F32).at[:bsz].set(c).at[bsz].set(c_ctx)
    mods = _modulation(cond, mod_w, mod_b)

    def mod_parts(i, ctx_rows):
        m = mods[i, bsz:bsz + 1] if ctx_rows else mods[i, :bsz]
        m = jnp.broadcast_to(m, (bsz, 3 * d)).reshape(bsz, 1, 3 * d)
        return m[..., :d], m[..., d:2 * d], m[..., 2 * d:]

    def in_proj(t, i, is_ctx, w, slot, **cols):
        shift, scale, _ = mod_parts(i, is_ctx)
        if is_ctx:
            flat = t.reshape(1, bsz * lc, d)
            return _mod_matmul(flat, scale[:1], shift[:1], w, slot, **cols).reshape(bsz, lc, -1)
        return _mod_matmul(t, scale, shift, w, slot, **cols)

    aw = N_HEADS * HEAD_DIM
    kw = N_KV_HEADS * HEAD_DIM
    for i in range(DEPTH):
        kind = kinds[i]
        slot = kinds[:i].count(kind)
        ctx_read_later = 2 in kinds[i + 1:]
        streams = [(x, False)] + ([(ctx, True)] if ctx_read_later else [])
        outs = []
        if kind == 0:
            w_out = a_w_out[slot].astype(BF16)
            w_s = a_w_s[slot].astype(BF16)
            b_s_t = a_b_s[slot].T
            for t, is_ctx in streams:
                gate = mod_parts(i, is_ctx)[2]
                uvg = in_proj(t, i, is_ctx, a_w_in, slot)
                z = _sgu_gate(uvg, a_ln_g[slot], a_ln_b[slot], w_s, b_s_t)
                outs.append(_out_matmul_norm(z, w_out, t, gate, post_g[i], post_b[i]))
        elif kind == 1:
            w_out = b_w_out[slot].astype(BF16)
            for t, is_ctx in streams:
                gate = mod_parts(i, is_ctx)[2]
                abg = in_proj(t, i, is_ctx, b_w_in, slot)
                z = _conv_gate(abg, b_conv_w[slot], b_conv_b[slot], b_ln_g[slot], b_ln_b[slot])
                outs.append(_out_matmul_norm(z, w_out, t, gate, post_g[i], post_b[i]))
        else:
            assert not ctx_read_later, "no later layer reads context after the attention layer at this depth"
            w_out = c_w_out[slot].astype(BF16)
            gate = mod_parts(i, False)[2]
            proj = in_proj(x, i, False, c_w_in, slot)
            proj_c = in_proj(ctx, i, True, c_w_in, slot, col0=aw, n=2 * kw)
            cosf, sinf = _rope_tables(n_tokens)
            q, k, v = _qk_norm_rope(proj, cosf, sinf, c_q_g[slot], c_k_g[slot], True)
            kc, vc = _qk_norm_rope(proj_c, jnp.ones((lc, HEAD_DIM), F32), jnp.zeros((lc, HEAD_DIM), F32),
                                   c_q_g[slot], c_k_g[slot], False)
            k_all = jnp.concatenate([k, kc], axis=1)
            v_all = jnp.concatenate([v, vc], axis=1)
            z = _attention(q, k_all, v_all, proj)
            outs.append(_out_matmul_norm(z, w_out, x, gate, post_g[i], post_b[i]))
        x = outs[0]
        if ctx_read_later:
            ctx = outs[1]
    return x
```

```python
import functools

import jax
import jax.numpy as jnp
from jax import lax
from jax.experimental import pallas as pl
from jax.experimental.pallas import tpu as pltpu

F32 = jnp.float32
BF16 = jnp.bfloat16

DEPTH = 4
N_MIXERS = 3
GRID_W = 64
CHUNK = 128
SGU_GROUPS = 16
CONV_W = 31
CONV_PAD = CONV_W // 2
HALO = 16
HEAD_DIM = 128
N_HEADS = 16
N_KV_HEADS = 4
GQA_GROUP = N_HEADS // N_KV_HEADS
ROPE_THETA = 10000.0
Q_SCALE = HEAD_DIM ** -0.5 * 1.4426950408889634
DEEPNORM_ALPHA = (2 * DEPTH) ** 0.25
LN_EPS = 1e-6
MOD_ROWS = 8
NORM_ROWS = 32

VMEM_LIMIT = 56 * 1024 * 1024


def _cparams(sem):
    return pltpu.CompilerParams(dimension_semantics=sem, vmem_limit_bytes=VMEM_LIMIT)


def _layer_norm(x, g, b):
    mu = jnp.mean(x, axis=-1, keepdims=True)
    xc = x - mu
    var = jnp.mean(xc * xc, axis=-1, keepdims=True)
    return xc * lax.rsqrt(var + LN_EPS) * g + b


def _silu(x):
    return x * jax.nn.sigmoid(x)


def _mod_kernel(c_ref, w_ref, b_ref, o_ref):
    s = _silu(c_ref[...]).astype(BF16)
    o_ref[0] = jnp.dot(s, w_ref[0].astype(BF16), preferred_element_type=F32) + b_ref[0]


def _modulation(cond, mod_w, mod_b):
    depth, d, n = mod_w.shape
    bn = 1024
    return pl.pallas_call(
        _mod_kernel,
        grid=(depth, n // bn),
        in_specs=[
            pl.BlockSpec((MOD_ROWS, d), lambda i, j: (0, 0)),
            pl.BlockSpec((1, d, bn), lambda i, j: (i, 0, j)),
            pl.BlockSpec((1, 1, bn), lambda i, j: (i, 0, j)),
        ],
        out_specs=pl.BlockSpec((1, MOD_ROWS, bn), lambda i, j: (i, 0, j)),
        out_shape=jax.ShapeDtypeStruct((depth, MOD_ROWS, n), F32),
        compiler_params=_cparams(("arbitrary", "arbitrary")),
        name="modulation",
    )(cond, mod_w, mod_b.reshape(depth, 1, n))


def _mm_kernel(x_ref, sc_ref, sh_ref, w_ref, o_ref, h_ref):
    @pl.when(pl.program_id(2) == 0)
    def _():
        h_ref[...] = (x_ref[0] * (1.0 + sc_ref[0]) + sh_ref[0]).astype(BF16)

    o_ref[0] = jnp.dot(h_ref[...], w_ref[0].astype(BF16), preferred_element_type=F32).astype(o_ref.dtype)


def _mod_matmul(x, scale, shift, w, slot, col0=0, n=None, bm=1024, bn=1024):
    bsz, s, d = x.shape
    n = w.shape[2] if n is None else n
    bm = min(bm, s)
    bn = min(bn, n)
    j0 = col0 // bn
    assert col0 % bn == 0 and n % bn == 0 and s % bm == 0
    return pl.pallas_call(
        _mm_kernel,
        grid=(bsz, s // bm, n // bn),
        in_specs=[
            pl.BlockSpec((1, bm, d), lambda b, i, j: (b, i, 0)),
            pl.BlockSpec((1, 1, d), lambda b, i, j: (b, 0, 0)),
            pl.BlockSpec((1, 1, d), lambda b, i, j: (b, 0, 0)),
            pl.BlockSpec((1, d, bn), lambda b, i, j: (slot, 0, j0 + j)),
        ],
        out_specs=pl.BlockSpec((1, bm, bn), lambda b, i, j: (b, i, j)),
        out_shape=jax.ShapeDtypeStruct((bsz, s, n), BF16),
        scratch_shapes=[pltpu.VMEM((bm, d), BF16)],
        compiler_params=_cparams(("arbitrary", "arbitrary", "arbitrary")),
        name="mod_matmul",
    )(x, scale, shift, w)


OUT_SUB_ROWS = 256


def _out_kernel(z_ref, w_ref, x_ref, gate_ref, pg_ref, pb_ref, o_ref):
    bm = o_ref.shape[1]
    sub = min(OUT_SUB_ROWS, bm)
    for r0 in range(0, bm, sub):
        rows = slice(r0, r0 + sub)
        y = jnp.dot(z_ref[0, rows, :], w_ref[...], preferred_element_type=F32)
        o_ref[0, rows, :] = DEEPNORM_ALPHA * x_ref[0, rows, :] + gate_ref[0] * y
        for t0 in range(r0, r0 + sub, NORM_ROWS):
            tile = slice(t0, t0 + NORM_ROWS)
            o_ref[0, tile, :] = _layer_norm(o_ref[0, tile, :], pg_ref[...], pb_ref[...])


def _out_matmul_norm(z, w, x, gate, post_g, post_b, bm=512):
    bsz, s, kdim = z.shape
    d = w.shape[1]
    bm = min(bm, s)
    return pl.pallas_call(
        _out_kernel,
        grid=(bsz, s // bm),
        in_specs=[
            pl.BlockSpec((1, bm, kdim), lambda b, i: (b, i, 0)),
            pl.BlockSpec((kdim, d), lambda b, i: (0, 0), pipeline_mode=pl.Buffered(1)),
            pl.BlockSpec((1, bm, d), lambda b, i: (b, i, 0)),
            pl.BlockSpec((1, 1, d), lambda b, i: (b, 0, 0)),
            pl.BlockSpec((1, d), lambda b, i: (0, 0)),
            pl.BlockSpec((1, d), lambda b, i: (0, 0)),
        ],
        out_specs=pl.BlockSpec((1, bm, d), lambda b, i: (b, i, 0)),
        out_shape=jax.ShapeDtypeStruct((bsz, s, d), F32),
        compiler_params=_cparams(("arbitrary", "arbitrary")),
        name="out_matmul_norm",
    )(z, w, x, gate, post_g.reshape(1, d), post_b.reshape(1, d))


def _sgu_kernel(u_ref, v_ref, g_ref, lg_ref, lb_ref, ws_ref, bs_ref, z_ref, vn_ref):
    bm = v_ref.shape[1]
    gw = v_ref.shape[2] // SGU_GROUPS
    for c in range(bm // CHUNK):
        rows = slice(c * CHUNK, (c + 1) * CHUNK)
        vn_ref[...] = _layer_norm(v_ref[0, rows, :].astype(F32), lg_ref[...], lb_ref[...]).astype(BF16)
        for h in range(SGU_GROUPS):
            cols = slice(h * gw, (h + 1) * gw)
            s = jnp.dot(ws_ref[h], vn_ref[:, cols], preferred_element_type=F32) + bs_ref[:, h:h + 1]
            u = u_ref[0, rows, cols].astype(F32)
            z_ref[0, rows, cols] = (u * s * _silu(g_ref[0, rows, cols].astype(F32))).astype(BF16)


def _sgu_gate(uvg, ln_g, ln_b, w_s, b_s_t, bm=512):
    bsz, s, n3 = uvg.shape
    e = n3 // 3
    bm = min(bm, s)
    return pl.pallas_call(
        _sgu_kernel,
        grid=(bsz, s // bm),
        in_specs=[
            pl.BlockSpec((1, bm, e), lambda b, i: (b, i, 0)),
            pl.BlockSpec((1, bm, e), lambda b, i: (b, i, 1)),
            pl.BlockSpec((1, bm, e), lambda b, i: (b, i, 2)),
            pl.BlockSpec((1, e), lambda b, i: (0, 0)),
            pl.BlockSpec((1, e), lambda b, i: (0, 0)),
            pl.BlockSpec((SGU_GROUPS, CHUNK, CHUNK), lambda b, i: (0, 0, 0)),
            pl.BlockSpec((CHUNK, SGU_GROUPS), lambda b, i: (0, 0)),
        ],
        out_specs=pl.BlockSpec((1, bm, e), lambda b, i: (b, i, 0)),
        out_shape=jax.ShapeDtypeStruct((bsz, s, e), BF16),
        scratch_shapes=[pltpu.VMEM((CHUNK, e), BF16)],
        compiler_params=_cparams(("arbitrary", "arbitrary")),
        name="sgu_gate",
    )(uvg, uvg, uvg, ln_g.reshape(1, e), ln_b.reshape(1, e), w_s, b_s_t)


CONV_ROWS = 128
CONV_COLS = 128
SUBLANES = 8


def _glu(a, b):
    return a.astype(F32) * jax.nn.sigmoid(b.astype(F32))


def _conv_kernel(a_ref, b_ref, g_ref, ap_ref, bp_ref, an_ref, bn_ref,
                 cw_ref, cb_ref, lg_ref, lb_ref, z_ref, yext_ref, yc_ref):
    i = pl.program_id(1)
    bm = a_ref.shape[1]
    e = a_ref.shape[2]
    yext_ref[HALO:HALO + bm, :] = _glu(a_ref[0], b_ref[0])
    yext_ref[0:HALO, :] = jnp.where(i > 0, _glu(ap_ref[0], bp_ref[0]), 0.0)
    yext_ref[HALO + bm:HALO + bm + HALO, :] = jnp.where(
        i < pl.num_programs(1) - 1, _glu(an_ref[0], bn_ref[0]), 0.0)

    assert HALO - CONV_PAD == 1
    rc = min(CONV_ROWS, bm)
    for r0 in range(0, bm, rc):
        for c0 in range(0, e, CONV_COLS):
            cols = slice(c0, c0 + CONV_COLS)
            acc = jnp.broadcast_to(cb_ref[:, cols], (rc, CONV_COLS))
            for r in range(SUBLANES):
                part = None
                for j in range(r if r else SUBLANES, CONV_W + 1, SUBLANES):
                    lo = r0 + j - r
                    term = cw_ref[j - 1:j, cols] * yext_ref[lo:lo + rc + SUBLANES, cols]
                    part = term if part is None else part + term
                acc = acc + part[r:r + rc]
            yc_ref[r0:r0 + rc, cols] = acc

    def norm_tile(t, carry):
        rows = pl.ds(pl.multiple_of(t * NORM_ROWS, NORM_ROWS), NORM_ROWS)
        y = _silu(_layer_norm(yc_ref[rows, :], lg_ref[...], lb_ref[...]))
        z_ref[0, rows, :] = (y * _silu(g_ref[0, rows, :].astype(F32))).astype(BF16)
        return carry

    lax.fori_loop(0, bm // NORM_ROWS, norm_tile, 0)


def _conv_gate(abg, conv_w, conv_b, ln_g, ln_b, bm=256):
    bsz, s, n3 = abg.shape
    e = n3 // 3
    bm = min(bm, s)
    hb = bm // HALO
    last = s // HALO - 1

    def prev_map(col):
        return lambda b, i: (b, jnp.maximum(i * hb - 1, 0), col)

    def next_map(col):
        return lambda b, i: (b, jnp.minimum((i + 1) * hb, last), col)

    return pl.pallas_call(
        _conv_kernel,
        grid=(bsz, s // bm),
        in_specs=[
            pl.BlockSpec((1, bm, e), lambda b, i: (b, i, 0)),
            pl.BlockSpec((1, bm, e), lambda b, i: (b, i, 1)),
            pl.BlockSpec((1, bm, e), lambda b, i: (b, i, 2)),
            pl.BlockSpec((1, HALO, e), prev_map(0)),
            pl.BlockSpec((1, HALO, e), prev_map(1)),
            pl.BlockSpec((1, HALO, e), next_map(0)),
            pl.BlockSpec((1, HALO, e), next_map(1)),
            pl.BlockSpec((CONV_W, e), lambda b, i: (0, 0)),
            pl.BlockSpec((1, e), lambda b, i: (0, 0)),
            pl.BlockSpec((1, e), lambda b, i: (0, 0)),
            pl.BlockSpec((1, e), lambda b, i: (0, 0)),
        ],
        out_specs=pl.BlockSpec((1, bm, e), lambda b, i: (b, i, 0)),
        out_shape=jax.ShapeDtypeStruct((bsz, s, e), BF16),
        scratch_shapes=[pltpu.VMEM((bm + 2 * HALO, e), F32), pltpu.VMEM((bm, e), F32)],
        compiler_params=_cparams(("arbitrary", "arbitrary")),
        name="conv_gate",
    )(abg, abg, abg, abg, abg, abg, abg, conv_w, conv_b.reshape(1, e),
      ln_g.reshape(1, e), ln_b.reshape(1, e))


def _norm_rope_head(t, gain, cosf, sinf, even):
    ones = jnp.ones((HEAD_DIM, HEAD_DIM), BF16)
    ms = jnp.dot((t * t).astype(BF16), ones, preferred_element_type=F32) * (1.0 / HEAD_DIM)
    tn = t * lax.rsqrt(ms + LN_EPS) * gain
    swapped = jnp.where(even, pltpu.roll(tn, HEAD_DIM - 1, 1), pltpu.roll(tn, 1, 1))
    return tn * cosf + swapped * sinf


def _qk_kernel(nq, *refs):
    if nq:
        q_ref, k_ref, v_ref, cos_ref, sin_ref, qg_ref, kg_ref, qo_ref, ko_ref, vo_ref = refs
    else:
        k_ref, v_ref, cos_ref, sin_ref, qg_ref, kg_ref, ko_ref, vo_ref = refs
    cosf = cos_ref[...]
    sinf = sin_ref[...]
    even = (lax.broadcasted_iota(jnp.int32, cosf.shape, 1) % 2) == 0
    for h in range(nq):
        cols = slice(h * HEAD_DIM, (h + 1) * HEAD_DIM)
        qh = _norm_rope_head(q_ref[0, :, cols].astype(F32), qg_ref[...], cosf, sinf, even)
        qo_ref[0, :, cols] = (qh * Q_SCALE).astype(BF16)
    for h in range(N_KV_HEADS):
        cols = slice(h * HEAD_DIM, (h + 1) * HEAD_DIM)
        kh = _norm_rope_head(k_ref[0, :, cols].astype(F32), kg_ref[...], cosf, sinf, even)
        ko_ref[0, :, cols] = kh.astype(BF16)
    vo_ref[0] = v_ref[0]


def _qk_norm_rope(proj, cosf, sinf, q_g, k_g, with_q, bm=256):
    bsz, s, _ = proj.shape
    bm = min(bm, s)
    aw = N_HEADS * HEAD_DIM
    kw = N_KV_HEADS * HEAD_DIM
    koff = (aw // kw) if with_q else 0
    in_specs = []
    out_specs = []
    out_shape = []
    args = []
    if with_q:
        in_specs.append(pl.BlockSpec((1, bm, aw), lambda b, i: (b, i, 0)))
        args.append(proj)
        out_specs.append(pl.BlockSpec((1, bm, aw), lambda b, i: (b, i, 0)))
        out_shape.append(jax.ShapeDtypeStruct((bsz, s, aw), BF16))
    in_specs += [
        pl.BlockSpec((1, bm, kw), lambda b, i: (b, i, koff)),
        pl.BlockSpec((1, bm, kw), lambda b, i: (b, i, koff + 1)),
        pl.BlockSpec((bm, HEAD_DIM), lambda b, i: (i, 0)),
        pl.BlockSpec((bm, HEAD_DIM), lambda b, i: (i, 0)),
        pl.BlockSpec((1, HEAD_DIM), lambda b, i: (0, 0)),
        pl.BlockSpec((1, HEAD_DIM), lambda b, i: (0, 0)),
    ]
    args += [proj, proj, cosf, sinf, q_g.reshape(1, HEAD_DIM), k_g.reshape(1, HEAD_DIM)]
    out_specs += [pl.BlockSpec((1, bm, kw), lambda b, i: (b, i, 0))] * 2
    out_shape += [jax.ShapeDtypeStruct((bsz, s, kw), BF16)] * 2
    return pl.pallas_call(
        functools.partial(_qk_kernel, N_HEADS if with_q else 0),
        grid=(bsz, s // bm),
        in_specs=in_specs,
        out_specs=out_specs,
        out_shape=out_shape,
        compiler_params=_cparams(("arbitrary", "arbitrary")),
        name="qk_norm_rope",
    )(*args)


def _attn_kernel(q_ref, k_ref, v_ref, g_ref, z_ref):
    k = k_ref[0]
    vx = jnp.concatenate([v_ref[0], jnp.ones_like(v_ref[0])], axis=1)
    for h in range(GQA_GROUP):
        cols = slice(h * HEAD_DIM, (h + 1) * HEAD_DIM)
        s = lax.dot_general(q_ref[0, :, cols], k, (((1,), (1,)), ((), ())), preferred_element_type=F32)
        m = jnp.max(s, axis=-1, keepdims=True)
        p = jnp.exp2((s - m).astype(BF16))
        ox = jnp.dot(p, vx, preferred_element_type=F32)
        o = ox[:, :HEAD_DIM] / ox[:, HEAD_DIM:]
        z_ref[0, :, cols] = (o * _silu(g_ref[0, :, cols].astype(F32))).astype(BF16)


def _attention(q, k_all, v_all, proj, bq=256):
    bsz, s, aw = q.shape
    lk = k_all.shape[1]
    gw = GQA_GROUP * HEAD_DIM
    goff = (proj.shape[2] - aw) // gw
    bq = min(bq, s)
    return pl.pallas_call(
        _attn_kernel,
        grid=(bsz, N_KV_HEADS, s // bq),
        in_specs=[
            pl.BlockSpec((1, bq, gw), lambda b, h, i: (b, i, h)),
            pl.BlockSpec((1, lk, HEAD_DIM), lambda b, h, i: (b, 0, h)),
            pl.BlockSpec((1, lk, HEAD_DIM), lambda b, h, i: (b, 0, h)),
            pl.BlockSpec((1, bq, gw), lambda b, h, i: (b, i, goff + h)),
        ],
        out_specs=pl.BlockSpec((1, bq, gw), lambda b, h, i: (b, i, h)),
        out_shape=jax.ShapeDtypeStruct((bsz, s, aw), BF16),
        compiler_params=_cparams(("arbitrary", "arbitrary", "arbitrary")),
        name="gqa_attention",
    )(q, k_all, v_all, proj)


def _rope_tables(n_tokens):
    rows = n_tokens // GRID_W
    row = jnp.repeat(jnp.arange(rows, dtype=F32), GRID_W)
    col = jnp.tile(jnp.arange(GRID_W, dtype=F32), rows)
    axis_dim = HEAD_DIM // 2
    inv = 1.0 / (ROPE_THETA ** (jnp.arange(0, axis_dim, 2, dtype=F32) / axis_dim))
    ang = jnp.concatenate([row[:, None] * inv, col[:, None] * inv], axis=-1)
    cosf = jnp.repeat(jnp.cos(ang), 2, axis=-1)
    sinf = jnp.repeat(jnp.sin(ang), 2, axis=-1) * jnp.tile(jnp.array([-1.0, 1.0], F32), HEAD_DIM // 2)
    return cosf, sinf


def kernel(x, c, ctx, c_ctx, mod_w, mod_b, post_g, post_b, a_w_in, a_ln_g, a_ln_b, a_w_s, a_b_s, a_w_out, b_w_in, b_conv_w, b_conv_b, b_ln_g, b_ln_b, b_w_out, c_w_in, c_q_g, c_k_g, c_w_out):
    bsz, n_tokens, d = x.shape
    lc = ctx.shape[1]
    assert bsz + 1 <= MOD_ROWS
    kinds = [i % N_MIXERS for i in range(DEPTH)]

    cond = jnp.zeros((MOD_ROWS, d), F32).at[:bsz].set(c).at[bsz].set(c_ctx)
    mods = _modulation(cond, mod_w, mod_b)

    def mod_parts(i, ctx_rows):
        m = mods[i, bsz:bsz + 1] if ctx_rows else mods[i, :bsz]
        m = jnp.broadcast_to(m, (bsz, 3 * d)).reshape(bsz, 1, 3 * d)
        return m[..., :d], m[..., d:2 * d], m[..., 2 * d:]

    def in_proj(t, i, is_ctx, w, slot, **cols):
        shift, scale, _ = mod_parts(i, is_ctx)
        if is_ctx:
            flat = t.reshape(1, bsz * lc, d)
            return _mod_matmul(flat, scale[:1], shift[:1], w, slot, **cols).reshape(bsz, lc, -1)
        return _mod_matmul(t, scale, shift, w, slot, **cols)

    aw = N_HEADS * HEAD_DIM
    kw = N_KV_HEADS * HEAD_DIM
    for i in range(DEPTH):
        kind = kinds[i]
        slot = kinds[:i].count(kind)
        ctx_read_later = 2 in kinds[i + 1:]
        streams = [(x, False)] + ([(ctx, True)] if ctx_read_later else [])
        outs = []
        if kind == 0:
            w_out = a_w_out[slot].astype(BF16)
            w_s = a_w_s[slot].astype(BF16)
            b_s_t = a_b_s[slot].T
            for t, is_ctx in streams:
                gate = mod_parts(i, is_ctx)[2]
                uvg = in_proj(t, i, is_ctx, a_w_in, slot)
                z = _sgu_gate(uvg, a_ln_g[slot], a_ln_b[slot], w_s, b_s_t)
                outs.append(_out_matmul_norm(z, w_out, t, gate, post_g[i], post_b[i]))
        elif kind == 1:
            w_out = b_w_out[slot].astype(BF16)
            for t, is_ctx in streams:
                gate = mod_parts(i, is_ctx)[2]
                abg = in_proj(t, i, is_ctx, b_w_in, slot)
                z = _conv_gate(abg, b_conv_w[slot], b_conv_b[slot], b_ln_g[slot], b_ln_b[slot])
                outs.append(_out_matmul_norm(z, w_out, t, gate, post_g[i], post_b[i]))
        else:
            assert not ctx_read_later, "no later layer reads context after the attention layer at this depth"
            w_out = c_w_out[slot].astype(BF16)
            gate = mod_parts(i, False)[2]
            proj = in_proj(x, i, False, c_w_in, slot)
            proj_c = in_proj(ctx, i, True, c_w_in, slot, col0=aw, n=2 * kw)
            cosf, sinf = _rope_tables(n_tokens)
            q, k, v = _qk_norm_rope(proj, cosf, sinf, c_q_g[slot], c_k_g[slot], True)
            kc, vc = _qk_norm_rope(proj_c, jnp.ones((lc, HEAD_DIM), F32), jnp.zeros((lc, HEAD_DIM), F32),
                                   c_q_g[slot], c_k_g[slot], False)
            k_all = jnp.concatenate([k, kc], axis=1)
            v_all = jnp.concatenate([v, vc], axis=1)
            z = _attention(q, k_all, v_all, proj)
            outs.append(_out_matmul_norm(z, w_out, x, gate, post_g[i], post_b[i]))
        x = outs[0]
        if ctx_read_later:
            ctx = outs[1]
    return x
```
